```python
import math
import jax, jax.numpy as jnp
from jax import lax
import numpy as np

D_MODEL = 1024
BATCH = 16
SEQ = 4096
DEPTH = 4

RET_HEADS = 4
RET_QK_DIM = 256
RET_V_DIM = 512
RET_QK = RET_HEADS * RET_QK_DIM
RET_V = RET_HEADS * RET_V_DIM
RET_CHUNK = 128
ROPE_THETA = 10000.0

SSD_INNER = 2 * D_MODEL
SSD_HEAD_DIM = 64
SSD_HEADS = SSD_INNER // SSD_HEAD_DIM
SSD_GROUPS = 8
SSD_HPG = SSD_HEADS // SSD_GROUPS
SSD_STATE = 128
SSD_CONV = 4
SSD_CHUNK = 128
SSD_CONV_DIM = SSD_INNER + 2 * SSD_GROUPS * SSD_STATE

IN_SPLITS = (RET_QK, RET_QK, RET_V, RET_V, SSD_INNER, SSD_CONV_DIM, SSD_HEADS, D_MODEL, D_MODEL)
IN_COLS = sum(IN_SPLITS)

FFN_DENSE = 2816
N_EXPERTS = 8
TOP_K = 2
FFN_EXPERT = 3584

DN_ALPHA = (2 * DEPTH) ** 0.25
DN_BETA = (8 * DEPTH) ** -0.25

LN_EPS = 1e-5
GN_EPS = 1e-5
RMS_EPS = 1e-5

kernel_name = "hybrid_retention_ssd_moe_deepnorm"


def layer_norm(x, g, b):
    xf = x.astype(jnp.float32)
    mu = jnp.mean(xf, axis=-1, keepdims=True)
    var = jnp.mean(jnp.square(xf - mu), axis=-1, keepdims=True)
    return ((xf - mu) * lax.rsqrt(var + LN_EPS) * g + b).astype(x.dtype)


def split_columns(proj):
    parts, start = [], 0
    for width in IN_SPLITS:
        parts.append(proj[..., start:start + width])
        start += width
    return parts


def rotary(t, positions):
    half = t.shape[-1] // 2
    inv_freq = ROPE_THETA ** (-jnp.arange(half, dtype=jnp.float32) / half)
    ang = positions.astype(jnp.float32)[..., None] * inv_freq
    cos = jnp.cos(ang)[:, :, None, :]
    sin = jnp.sin(ang)[:, :, None, :]
    t1, t2 = t[..., 0::2], t[..., 1::2]
    return jnp.stack([t1 * cos - t2 * sin, t1 * sin + t2 * cos], axis=-1).reshape(t.shape)


def to_chunks(t, c):
    b, s = t.shape[:2]
    return jnp.moveaxis(t.reshape(b, s // c, c, *t.shape[2:]), 1, 0)


def from_chunks(t):
    t = jnp.moveaxis(t, 0, 1)
    return t.reshape(t.shape[0], -1, *t.shape[3:])


def retention(q, k, v):
    c = RET_CHUNK
    log_gamma = jnp.log(1.0 - 2.0 ** (-5.0 - jnp.arange(RET_HEADS, dtype=jnp.float32)))
    pos = jnp.arange(c, dtype=jnp.float32)
    diff = pos[:, None] - pos[None, :]
    causal = diff >= 0
    intra_decay = jnp.where(causal[None], jnp.exp(log_gamma[:, None, None] * jnp.where(causal, diff, 0.0)[None]), 0.0)
    query_decay = jnp.exp(log_gamma[None, :] * (pos[:, None] + 1.0))
    key_decay = jnp.exp(log_gamma[None, :] * (c - 1.0 - pos[:, None]))
    chunk_decay = jnp.exp(log_gamma * c)
    k = k * RET_QK_DIM ** -0.5

    def step(state, qkv):
        qc, kc, vc = qkv
        scores = jnp.einsum("bthd,bshd->bhts", qc, kc) * intra_decay
        inner = jnp.einsum("bhts,bshv->bthv", scores, vc)
        cross = jnp.einsum("bthd,bhdv->bthv", qc, state) * query_decay[None, :, :, None]
        state = state * chunk_decay[None, :, None, None] + jnp.einsum(
            "bshd,bshv->bhdv", kc * key_decay[None, :, :, None], vc)
        return state, inner + cross

    state0 = jnp.zeros((q.shape[0], RET_HEADS, RET_QK_DIM, RET_V_DIM), jnp.float32)
    _, out = lax.scan(step, state0, (to_chunks(q, c), to_chunks(k, c), to_chunks(v, c)))
    return from_chunks(out)


def ssd_scan(xs, a, bm, cm):
    c = SSD_CHUNK
    causal = jnp.tril(jnp.ones((c, c), dtype=bool))[None, :, :, None, None]

    def step(state, inp):
        xc, ac, bc, cc = inp
        a_cs = jnp.cumsum(ac, axis=1)
        seg = a_cs[:, :, None] - a_cs[:, None, :]
        decay = jnp.exp(jnp.where(causal, seg, -jnp.inf))
        cb = jnp.einsum("btgn,bsgn->btsg", cc, bc)
        y_diag = jnp.einsum("btsg,btsgr,bsgrp->btgrp", cb, decay, xc)
        y_off = jnp.einsum("btgn,bgrpn->btgrp", cc, state) * jnp.exp(a_cs)[..., None]
        a_last = a_cs[:, -1]
        state = state * jnp.exp(a_last)[..., None, None] + jnp.einsum(
            "bsgn,bsgr,bsgrp->bgrpn", bc, jnp.exp(a_last[:, None] - a_cs), xc)
        return state, y_diag + y_off

    state0 = jnp.zeros((xs.shape[0], SSD_GROUPS, SSD_HPG, SSD_HEAD_DIM, SSD_STATE), jnp.float32)
    _, y = lax.scan(step, state0, (to_chunks(xs, c), to_chunks(a, c), to_chunks(bm, c), to_chunks(cm, c)))
    return from_chunks(y)


def causal_depthwise_conv(x, w, bias):
    out = lax.conv_general_dilated(
        x, w[:, None, :], window_strides=(1,), padding=[(SSD_CONV - 1, 0)],
        dimension_numbers=("NWC", "WIO", "NWC"), feature_group_count=x.shape[-1])
    return out + bias


def hybrid_mixer(x, positions, w_in, conv_w, conv_b, dt_bias, a_log, d_skip, ssd_norm_w,
                 ret_gn_w, ret_gn_b, w_ret_br, w_ssd_br, w_out):
    b, s, _ = x.shape
    f32 = jnp.float32
    proj = jnp.einsum("bsd,de->bse", x, w_in)
    q, k, v, g, z, xbc, dt, gate_ret, gate_ssd = split_columns(proj)

    q = rotary(q.astype(f32).reshape(b, s, RET_HEADS, RET_QK_DIM), positions)
    k = rotary(k.astype(f32).reshape(b, s, RET_HEADS, RET_QK_DIM), positions)
    v = v.astype(f32).reshape(b, s, RET_HEADS, RET_V_DIM)
    ret = retention(q, k, v)
    mu = jnp.mean(ret, axis=-1, keepdims=True)
    var = jnp.mean(jnp.square(ret - mu), axis=-1, keepdims=True)
    ret = ((ret - mu) * lax.rsqrt(var + GN_EPS)).reshape(b, s, RET_V) * ret_gn_w + ret_gn_b
    ret = (jax.nn.silu(g.astype(f32)) * ret).astype(x.dtype)
    y_ret = jnp.einsum("bse,ed->bsd", ret, w_ret_br)

    xbc = jax.nn.silu(causal_depthwise_conv(xbc, conv_w, conv_b)).astype(f32)
    xs = xbc[..., :SSD_INNER].reshape(b, s, SSD_GROUPS, SSD_HPG, SSD_HEAD_DIM)
    bm = xbc[..., SSD_INNER:SSD_INNER + SSD_GROUPS * SSD_STATE].reshape(b, s, SSD_GROUPS, SSD_STATE)
    cm = xbc[..., SSD_INNER + SSD_GROUPS * SSD_STATE:].reshape(b, s, SSD_GROUPS, SSD_STATE)
    dt = jax.nn.softplus(dt.astype(f32) + dt_bias).reshape(b, s, SSD_GROUPS, SSD_HPG)
    a = -jnp.exp(a_log.astype(f32)).reshape(SSD_GROUPS, SSD_HPG)
    y = ssd_scan(xs * dt[..., None], dt * a, bm, cm)
    y = y + d_skip.reshape(SSD_GROUPS, SSD_HPG)[:, :, None] * xs
    y = y.reshape(b, s, SSD_INNER) * jax.nn.silu(z.astype(f32))
    yg = y.reshape(b, s, SSD_GROUPS, SSD_INNER // SSD_GROUPS)
    yg = yg * lax.rsqrt(jnp.mean(jnp.square(yg), axis=-1, keepdims=True) + RMS_EPS)
    y = (yg.reshape(b, s, SSD_INNER) * ssd_norm_w).astype(x.dtype)
    y_ssd = jnp.einsum("bse,ed->bsd", y, w_ssd_br)

    merged = jax.nn.sigmoid(gate_ret) * y_ret + jax.nn.sigmoid(gate_ssd) * y_ssd
    return jnp.einsum("bsd,de->bse", merged, w_out)


def swiglu(x, w_gate, w_up, w_down):
    h = jax.nn.silu(jnp.einsum("bsd,df->bsf", x, w_gate)) * jnp.einsum("bsd,df->bsf", x, w_up)
    return jnp.einsum("bsf,fd->bsd", h, w_down)


def moe_swiglu(x, router, w_gate, w_up, w_down):
    logits = jnp.einsum("bsd,de->bse", x, router).astype(jnp.float32)
    top_val, top_idx = lax.top_k(logits, TOP_K)
    top_w = jax.nn.softmax(top_val, axis=-1)
    gates = jnp.sum(jax.nn.one_hot(top_idx, N_EXPERTS, dtype=jnp.float32) * top_w[..., None], axis=-2)
    gates = gates.astype(x.dtype)
    out = jnp.zeros_like(x)
    for e in range(N_EXPERTS):
        out = out + gates[..., e:e + 1] * swiglu(x, w_gate[e], w_up[e], w_down[e])
    return out


def setup_inputs(seed: int = 0) -> dict:
    key = jax.random.key(seed)
    ks = jax.random.split(key, 26)
    f32 = jnp.float32
    L = DEPTH
    nd = (DEPTH + 1) // 2
    nm = DEPTH // 2
    D = D_MODEL

    def normal(k, shape, scale):
        return jax.random.normal(k, shape, f32) * scale

    x = normal(ks[0], (BATCH, SEQ, D), 1.0)
    start = jax.random.randint(ks[1], (BATCH, 1), 0, 1024, dtype=jnp.int32)
    positions = start + jnp.arange(SEQ, dtype=jnp.int32)[None, :]
    dt_off = sum(IN_SPLITS[:6])
    w_in = normal(ks[2], (L, D, IN_COLS), D ** -0.5)
    w_in = w_in.at[:, :, dt_off:dt_off + SSD_HEADS].multiply(0.1)
    conv_w = normal(ks[3], (L, SSD_CONV, SSD_CONV_DIM), SSD_CONV ** -0.5)
    conv_b = normal(ks[4], (L, SSD_CONV_DIM), 0.01)
    dt0 = jnp.exp(jax.random.uniform(ks[5], (L, SSD_HEADS), f32, math.log(1e-3), math.log(1e-1)))
    dt_bias = dt0 + jnp.log(-jnp.expm1(-dt0))
    a_log = jnp.log(jax.random.uniform(ks[6], (L, SSD_HEADS), f32, 1.0, 16.0))
    d_skip = 1.0 + normal(ks[7], (L, SSD_HEADS), 0.1)
    ssd_norm_w = 1.0 + normal(ks[8], (L, SSD_INNER), 0.05)
    ret_gn_w = 1.0 + normal(ks[9], (L, RET_V), 0.05)
    ret_gn_b = normal(ks[10], (L, RET_V), 0.01)
    w_ret_br = normal(ks[11], (L, RET_V, D), RET_V ** -0.5)
    w_ssd_br = normal(ks[12], (L, SSD_INNER, D), SSD_INNER ** -0.5)
    w_out = normal(ks[13], (L, D, D), DN_BETA * D ** -0.5)
    ln1_g = 1.0 + normal(ks[14], (L, D), 0.05)
    ln1_b = normal(ks[15], (L, D), 0.01)
    ln2_g = 1.0 + normal(ks[16], (L, D), 0.05)
    ln2_b = normal(ks[17], (L, D), 0.01)
    dense_w_gate = normal(ks[18], (nd, D, FFN_DENSE), D ** -0.5)
    dense_w_up = normal(ks[19], (nd, D, FFN_DENSE), D ** -0.5)
    dense_w_down = normal(ks[20], (nd, FFN_DENSE, D), DN_BETA * FFN_DENSE ** -0.5)
    moe_router = normal(ks[21], (nm, D, N_EXPERTS), D ** -0.5)
    moe_w_gate = normal(ks[22], (nm, N_EXPERTS, D, FFN_EXPERT), D ** -0.5)
    moe_w_up = normal(ks[23], (nm, N_EXPERTS, D, FFN_EXPERT), D ** -0.5)
    moe_w_down = normal(ks[24], (nm, N_EXPERTS, FFN_EXPERT, D), DN_BETA * FFN_EXPERT ** -0.5)
    return {"x": x, "positions": positions, "w_in": w_in, "conv_w": conv_w, "conv_b": conv_b,
            "dt_bias": dt_bias, "a_log": a_log, "d_skip": d_skip, "ssd_norm_w": ssd_norm_w,
            "ret_gn_w": ret_gn_w, "ret_gn_b": ret_gn_b, "w_ret_br": w_ret_br, "w_ssd_br": w_ssd_br,
            "w_out": w_out, "ln1_g": ln1_g, "ln1_b": ln1_b, "ln2_g": ln2_g, "ln2_b": ln2_b,
            "dense_w_gate": dense_w_gate, "dense_w_up": dense_w_up, "dense_w_down": dense_w_down,
            "moe_router": moe_router, "moe_w_gate": moe_w_gate, "moe_w_up": moe_w_up,
            "moe_w_down": moe_w_down}


def reference(x, positions, w_in, conv_w, conv_b, dt_bias, a_log, d_skip, ssd_norm_w,
              ret_gn_w, ret_gn_b, w_ret_br, w_ssd_br, w_out, ln1_g, ln1_b, ln2_g, ln2_b,
              dense_w_gate, dense_w_up, dense_w_down, moe_router, moe_w_gate, moe_w_up,
              moe_w_down):
    for layer in range(DEPTH):
        m = hybrid_mixer(x, positions, w_in[layer], conv_w[layer], conv_b[layer], dt_bias[layer],
                         a_log[layer], d_skip[layer], ssd_norm_w[layer], ret_gn_w[layer],
                         ret_gn_b[layer], w_ret_br[layer], w_ssd_br[layer], w_out[layer])
        x = layer_norm(DN_ALPHA * x + m, ln1_g[layer], ln1_b[layer])
        i = layer // 2
        if layer % 2 == 0:
            f = swiglu(x, dense_w_gate[i], dense_w_up[i], dense_w_down[i])
        else:
            f = moe_swiglu(x, moe_router[i], moe_w_gate[i], moe_w_up[i], moe_w_down[i])
        x = layer_norm(DN_ALPHA * x + f, ln2_g[layer], ln2_b[layer])
    return x
```

```python
import functools
import math

import jax
import jax.numpy as jnp
from jax import lax
from jax.experimental import pallas as pl
from jax.experimental.pallas import tpu as pltpu

F32 = jnp.float32
BF16 = jnp.bfloat16

D_MODEL = 1024
DEPTH = 4

RET_HEADS = 4
RET_DK = 256
RET_DV = 512
RET_QK = RET_HEADS * RET_DK
RET_V = RET_HEADS * RET_DV
RET_HEAD_COLS = 2 * RET_DK + 2 * RET_DV
ROPE_THETA = 10000.0

SSD_INNER = 2048
SSD_HEAD_DIM = 64
SSD_HEADS = 32
SSD_GROUPS = 8
SSD_STATE = 128
SSD_CONV = 4
SSD_CONV_DIM = SSD_INNER + 2 * SSD_GROUPS * SSD_STATE
SSD_PAIRS = SSD_HEADS // 2
SSD_GROUP_COLS = SSD_INNER // SSD_GROUPS

IN_SPLITS = (RET_QK, RET_QK, RET_V, RET_V, SSD_INNER, SSD_CONV_DIM, SSD_HEADS, D_MODEL, D_MODEL)

N_EXPERTS = 8
DN_ALPHA = (2 * DEPTH) ** 0.25
LN_EPS = 1e-5
GN_EPS = 1e-5
RMS_EPS = 1e-5

LANES = 128
CONV_HALO = 8
VMEM_LIMIT_BYTES = 56 * 1024 * 1024

RET_TILE = 512
RET_CHUNK = 256
SSD_TILE = 256
SSD_CHUNK = 128
FFN_TILE = 512
MOE_FT = 1792


def _dot(a, b):
    return jnp.dot(a, b, preferred_element_type=F32)


def _dot_nt(a, b):
    return lax.dot_general(a, b, (((1,), (1,)), ((), ())), preferred_element_type=F32)


def _dot_tn(a, b):
    return lax.dot_general(a, b, (((0,), (0,)), ((), ())), preferred_element_type=F32)


def _sigmoid(v):
    return 1.0 / (1.0 + jnp.exp(-v))


def _silu(v):
    return v * _sigmoid(v)


def _layer_norm(v, g, b):
    mu = jnp.mean(v, axis=-1, keepdims=True)
    d = v - mu
    var = jnp.mean(d * d, axis=-1, keepdims=True)
    return d * lax.rsqrt(var + LN_EPS) * g + b


def _split3(v):
    hi = v.astype(BF16)
    r1 = v - hi.astype(F32)
    mid = r1.astype(BF16)
    lo = (r1 - mid.astype(F32)).astype(BF16)
    return hi, mid, lo


def _const_spec(shape):
    nd = len(shape)
    return pl.BlockSpec(shape, lambda *_: (0,) * nd, pipeline_mode=pl.Buffered(1))


def _ret_kernel(x_ref, pos_ref, invf_ref, w_ref, wg_ref, wbr_ref, gnw_ref, gnb_ref,
                intra_ref, qdec_ref, kdec_ref, cdec_ref, out_ref,
                state_ref, xb_ref, cs_ref, proj_ref, ret_ref, acc_ref, *, tile, chunk):
    half = RET_DK // 2

    @pl.when(pl.program_id(1) == 0)
    def _():
        state_ref[...] = jnp.zeros_like(state_ref)

    xb_ref[...] = x_ref[0].astype(BF16)
    ang = pos_ref[0].astype(F32) * invf_ref[...]
    cs_ref[0] = jnp.cos(ang)
    cs_ref[1] = jnp.sin(ang)
    acc_ref[...] = jnp.zeros_like(acc_ref)

    def head(h, carry):
        proj_ref[...] = _dot(xb_ref[...], w_ref[h])
        for ci in range(tile // chunk):
            rows = slice(ci * chunk, (ci + 1) * chunk)
            cos = cs_ref[0, rows, :]
            sin = cs_ref[1, rows, :]
            q1 = proj_ref[rows, 0:half]
            q2 = proj_ref[rows, half:RET_DK]
            k1 = proj_ref[rows, RET_DK:RET_DK + half]
            k2 = proj_ref[rows, RET_DK + half:2 * RET_DK]
            v = proj_ref[rows, 2 * RET_DK:2 * RET_DK + RET_DV].astype(BF16)
            g = proj_ref[rows, 2 * RET_DK + RET_DV:RET_HEAD_COLS]
            q = jnp.concatenate([q1 * cos - q2 * sin, q1 * sin + q2 * cos], axis=1)
            k = jnp.concatenate([k1 * cos - k2 * sin, k1 * sin + k2 * cos], axis=1) * (RET_DK ** -0.5)
            scores = _dot_nt(q.astype(BF16), k.astype(BF16)) * intra_ref[h]
            inner = _dot(scores.astype(BF16), v)
            st = state_ref[h]
            cross = _dot((q * qdec_ref[h]).astype(BF16), st.astype(BF16))
            state_ref[h] = st * cdec_ref[h] + _dot_tn((k * kdec_ref[h]).astype(BF16), v)
            o = inner + cross
            mu = jnp.mean(o, axis=-1, keepdims=True)
            d = o - mu
            var = jnp.mean(d * d, axis=-1, keepdims=True)
            o = d * lax.rsqrt(var + GN_EPS) * gnw_ref[h] + gnb_ref[h]
            ret_ref[rows, :] = (_silu(g) * o).astype(BF16)
        acc_ref[...] += _dot(ret_ref[...], wbr_ref[h])
        return carry

    lax.fori_loop(0, RET_HEADS, head, 0)
    gate = _sigmoid(_dot(xb_ref[...], wg_ref[...]))
    out_ref[0] = gate * acc_ref[...]


def _ret_tables(chunk):
    log_gamma = jnp.log(1.0 - 2.0 ** (-5.0 - jnp.arange(RET_HEADS, dtype=F32)))
    pos = jnp.arange(chunk, dtype=F32)
    diff = pos[:, None] - pos[None, :]
    causal = diff >= 0
    intra = jnp.where(causal[None], jnp.exp(log_gamma[:, None, None] * jnp.where(causal, diff, 0.0)[None]), 0.0)
    qdec = jnp.exp(log_gamma[:, None] * (pos[None, :] + 1.0))
    kdec = jnp.exp(log_gamma[:, None] * (chunk - 1.0 - pos[None, :]))
    cdec = jnp.exp(log_gamma * chunk)
    qdec = jnp.broadcast_to(qdec[:, :, None], (RET_HEADS, chunk, RET_DK))
    kdec = jnp.broadcast_to(kdec[:, :, None], (RET_HEADS, chunk, RET_DK))
    cdec = jnp.broadcast_to(cdec[:, None, None], (RET_HEADS, 1, RET_DV))
    return intra, qdec, kdec, cdec


def _retention_branch(x, pos3, lw, tile, chunk):
    b, s, d = x.shape
    half = RET_DK // 2
    inv_freq = (ROPE_THETA ** (-jnp.arange(half, dtype=F32) / half)).reshape(1, half)
    intra, qdec, kdec, cdec = _ret_tables(chunk)
    kern = functools.partial(_ret_kernel, tile=tile, chunk=chunk)
    return pl.pallas_call(
        kern,
        grid=(b, s // tile),
        in_specs=[
            pl.BlockSpec((1, tile, d), lambda i, j: (i, j, 0)),
            pl.BlockSpec((1, tile, 1), lambda i, j: (i, j, 0)),
            _const_spec((1, half)),
            _const_spec((RET_HEADS, d, RET_HEAD_COLS)),
            _const_spec((d, d)),
            _const_spec((RET_HEADS, RET_DV, d)),
            _const_spec((RET_HEADS, 1, RET_DV)),
            _const_spec((RET_HEADS, 1, RET_DV)),
            _const_spec((RET_HEADS, chunk, chunk)),
            _const_spec((RET_HEADS, chunk, RET_DK)),
            _const_spec((RET_HEADS, chunk, RET_DK)),
            _const_spec((RET_HEADS, 1, RET_DV)),
        ],
        out_specs=pl.BlockSpec((1, tile, d), lambda i, j: (i, j, 0)),
        out_shape=jax.ShapeDtypeStruct((b, s, d), F32),
        scratch_shapes=[
            pltpu.VMEM((RET_HEADS, RET_DK, RET_DV), F32),
            pltpu.VMEM((tile, d), BF16),
            pltpu.VMEM((2, tile, half), F32),
            pltpu.VMEM((tile, RET_HEAD_COLS), F32),
            pltpu.VMEM((tile, RET_DV), BF16),
            pltpu.VMEM((tile, d), F32),
        ],
        compiler_params=pltpu.CompilerParams(
            dimension_semantics=("arbitrary", "arbitrary"), vmem_limit_bytes=VMEM_LIMIT_BYTES),
        name="retention_branch",
    )(x, pos3, inv_freq, lw["w_ret"], lw["w_gate_ret"], lw["w_ret_br"], lw["ret_gn_w"], lw["ret_gn_b"],
      intra, qdec, kdec, cdec)


def _ssd_kernel(x_ref, yret_ref, wz_ref, wxbc_ref, wdt_ref, wg_ref, wbr_ref, wout_ref,
                convw_ref, convb_ref, dtb_ref, alog_ref, dskip_ref, normw_ref, lng_ref, lnb_ref,
                out_ref, raw_ref, xbc_ref, z_ref, dt_ref, y_ref, state_ref, *, tile, chunk):
    @pl.when(pl.program_id(1) == 0)
    def _():
        raw_ref[0:CONV_HALO, :] = jnp.zeros((CONV_HALO, SSD_CONV_DIM), F32)
        state_ref[...] = jnp.zeros_like(state_ref)

    xb = x_ref[0].astype(BF16)
    raw_ref[CONV_HALO:CONV_HALO + tile, :] = _dot(xb, wxbc_ref[...])
    z_ref[...] = _dot(xb, wz_ref[...])
    dtr = _dot(xb, wdt_ref[...]) + dtb_ref[...]
    dt_ref[...] = jnp.maximum(dtr, 0.0) + jnp.log1p(jnp.exp(-jnp.abs(dtr)))

    conv = convb_ref[...]
    for k in range(SSD_CONV):
        off = CONV_HALO - (SSD_CONV - 1) + k
        conv = conv + convw_ref[k:k + 1, :] * raw_ref[off:off + tile, :]
    xbc_ref[...] = _silu(conv)
    raw_ref[0:CONV_HALO, :] = raw_ref[tile:tile + CONV_HALO, :]

    a_neg = -jnp.exp(alog_ref[...])
    rows_i = lax.broadcasted_iota(jnp.int32, (chunk, chunk), 0)
    cols_i = lax.broadcasted_iota(jnp.int32, (chunk, chunk), 1)
    tri = rows_i >= cols_i
    tril_bf = tri.astype(BF16)
    low_half = lax.broadcasted_iota(jnp.int32, (1, LANES), 1) < SSD_HEAD_DIM
    b_off = SSD_INNER
    c_off = SSD_INNER + SSD_GROUPS * SSD_STATE

    def chunk_body(ci, carry):
        r0 = pl.multiple_of(ci * chunk, chunk)
        rows = pl.ds(r0, chunk)
        dt_c = dt_ref[rows, :]
        hi, mid, lo = _split3(dt_c * a_neg)
        a_cs = _dot(tril_bf, hi) + _dot(tril_bf, mid) + _dot(tril_bf, lo)
        a_row = a_cs.T
        dt_row = dt_c.T
        wdt_row = jnp.exp(a_row[:, chunk - 1:chunk] - a_row) * dt_row
        for g in range(SSD_GROUPS):
            bc = xbc_ref[rows, b_off + g * SSD_STATE:b_off + (g + 1) * SSD_STATE]
            cc = xbc_ref[rows, c_off + g * SSD_STATE:c_off + (g + 1) * SSD_STATE]
            cbm = jnp.where(tri, _dot_nt(cc.astype(BF16), bc.astype(BF16)), 0.0)
            bct = bc.T
            ys = []
            for p in range(2):
                pair = 2 * g + p
                hd0 = 2 * pair
                hd1 = hd0 + 1
                xs_p = xbc_ref[rows, pair * LANES:(pair + 1) * LANES]
                col0 = jnp.broadcast_to(a_cs[:, hd0:hd0 + 1], (chunk, LANES))
                col1 = jnp.broadcast_to(a_cs[:, hd1:hd1 + 1], (chunk, LANES))
                l0 = cbm * jnp.exp(jnp.minimum(col0 - a_row[hd0:hd0 + 1, :], 0.0)) * dt_row[hd0:hd0 + 1, :]
                l1 = cbm * jnp.exp(jnp.minimum(col1 - a_row[hd1:hd1 + 1, :], 0.0)) * dt_row[hd1:hd1 + 1, :]
                e0 = jnp.exp(col0)
                e1 = jnp.exp(col1)
                lhs_y = jnp.concatenate([l0, l1, cc * e0, cc * e1], axis=1).astype(BF16)
                st = state_ref[pair]
                xs_lo = jnp.where(low_half, xs_p, 0.0)
                xs_hi = jnp.where(low_half, 0.0, xs_p)
                st_lo = jnp.where(low_half, st, 0.0)
                st_hi = jnp.where(low_half, 0.0, st)
                rhs = jnp.concatenate([xs_lo, xs_hi, st_lo, st_hi], axis=0).astype(BF16)
                y = _dot(lhs_y, rhs) + dskip_ref[:, pair * LANES:(pair + 1) * LANES] * xs_p
                lhs_s = jnp.concatenate(
                    [bct * wdt_row[hd0:hd0 + 1, :], bct * wdt_row[hd1:hd1 + 1, :]], axis=1).astype(BF16)
                e_last = jnp.where(low_half, e0[chunk - 1:chunk, :], e1[chunk - 1:chunk, :])
                state_ref[pair] = st * e_last + _dot(lhs_s, rhs[0:2 * chunk, :])
                ys.append(y)
            gcols = slice(g * SSD_GROUP_COLS, (g + 1) * SSD_GROUP_COLS)
            yg = jnp.concatenate(ys, axis=1) * _silu(z_ref[rows, gcols])
            ms = jnp.mean(yg * yg, axis=-1, keepdims=True)
            y_ref[rows, gcols] = (yg * lax.rsqrt(ms + RMS_EPS) * normw_ref[:, gcols]).astype(BF16)
        return carry

    lax.fori_loop(0, tile // chunk, chunk_body, 0)

    y_ssd = _dot(y_ref[...], wbr_ref[...])
    merged = _sigmoid(_dot(xb, wg_ref[...])) * y_ssd + yret_ref[0]
    mixed = _dot(merged.astype(BF16), wout_ref[...])
    out_ref[0] = _layer_norm(DN_ALPHA * x_ref[0] + mixed, lng_ref[...], lnb_ref[...])


def _ssd_branch(x, yret, lw, tile, chunk):
    b, s, d = x.shape
    kern = functools.partial(_ssd_kernel, tile=tile, chunk=chunk)
    tok = pl.BlockSpec((1, tile, d), lambda i, j: (i, j, 0))
    return pl.pallas_call(
        kern,
        grid=(b, s // tile),
        in_specs=[
            tok, tok,
            _const_spec((d, SSD_INNER)),
            _const_spec((d, SSD_CONV_DIM)),
            _const_spec((d, LANES)),
            _const_spec((d, d)),
            _const_spec((SSD_INNER, d)),
            _const_spec((d, d)),
            _const_spec((SSD_CONV, SSD_CONV_DIM)),
            _const_spec((1, SSD_CONV_DIM)),
            _const_spec((1, LANES)),
            _const_spec((1, LANES)),
            _const_spec((1, SSD_INNER)),
            _const_spec((1, SSD_INNER)),
            _const_spec((1, d)),
            _const_spec((1, d)),
        ],
        out_specs=tok,
        out_shape=jax.ShapeDtypeStruct((b, s, d), F32),
        scratch_shapes=[
            pltpu.VMEM((tile + CONV_HALO, SSD_CONV_DIM), F32),
            pltpu.VMEM((tile, SSD_CONV_DIM), F32),
            pltpu.VMEM((tile, SSD_INNER), F32),
            pltpu.VMEM((tile, LANES), F32),
            pltpu.VMEM((tile, SSD_INNER), BF16),
            pltpu.VMEM((SSD_PAIRS, SSD_STATE, LANES), F32),
        ],
        compiler_params=pltpu.CompilerParams(
            dimension_semantics=("arbitrary", "arbitrary"), vmem_limit_bytes=VMEM_LIMIT_BYTES),
        name="ssd_branch",
    )(x, yret, lw["w_z"], lw["w_xbc"], lw["w_dt"], lw["w_gate_ssd"], lw["w_ssd_br"], lw["w_out"],
      lw["conv_w"], lw["conv_b"], lw["dt_bias"], lw["a_log"], lw["d_skip"], lw["ssd_norm_w"],
      lw["ln1_g"], lw["ln1_b"])


def _dense_ffn_kernel(x_ref, wg_ref, wu_ref, wd_ref, lng_ref, lnb_ref, out_ref):
    x = x_ref[...]
    xb = x.astype(BF16)
    h = _silu(_dot(xb, wg_ref[...])) * _dot(xb, wu_ref[...])
    f = _dot(h.astype(BF16), wd_ref[...])
    out_ref[...] = _layer_norm(DN_ALPHA * x + f, lng_ref[...], lnb_ref[...])


def _dense_ffn(x2, w_gate, w_up, w_down, ln_g, ln_b, tile):
    n, d = x2.shape
    f = w_gate.shape[1]
    tok = pl.BlockSpec((tile, d), lambda i: (i, 0))
    return pl.pallas_call(
        _dense_ffn_kernel,
        grid=(n // tile,),
        in_specs=[tok, _const_spec((d, f)), _const_spec((d, f)), _const_spec((f, d)),
                  _const_spec((1, d)), _const_spec((1, d))],
        out_specs=tok,
        out_shape=jax.ShapeDtypeStruct((n, d), F32),
        compiler_params=pltpu.CompilerParams(
            dimension_semantics=("arbitrary",), vmem_limit_bytes=VMEM_LIMIT_BYTES),
        name="dense_ffn",
    )(x2, w_gate, w_up, w_down, ln_g, ln_b)


def _router_gates(x, router):
    xh, xm, xl = _split3(x)
    rh, rm, rl = _split3(router)
    logits = (_dot(xh, rh) + _dot(xh, rm) + _dot(xm, rh)) + (_dot(xh, rl) + _dot(xm, rm) + _dot(xl, rh))
    lane = lax.broadcasted_iota(jnp.int32, logits.shape, 1)
    neg = jnp.finfo(F32).min
    logits = jnp.where(lane < N_EXPERTS, logits, neg)
    m1 = jnp.max(logits, axis=-1, keepdims=True)
    i1 = jnp.min(jnp.where(logits == m1, lane, LANES), axis=-1, keepdims=True)
    rest = jnp.where(lane == i1, neg, logits)
    m2 = jnp.max(rest, axis=-1, keepdims=True)
    i2 = jnp.min(jnp.where(rest == m2, lane, LANES), axis=-1, keepdims=True)
    e = jnp.exp(m2 - m1)
    w1 = 1.0 / (1.0 + e)
    w2 = e / (1.0 + e)
    return jnp.where(lane == i1, w1, 0.0) + jnp.where(lane == i2, w2, 0.0)


def _moe_dense_kernel(x_ref, router_ref, wg_ref, wu_ref, wd_ref, lng_ref, lnb_ref, out_ref,
                      xb_ref, gates_ref, acc_ref):
    e = pl.program_id(1)
    f = pl.program_id(2)

    @pl.when((e == 0) & (f == 0))
    def _():
        x = x_ref[...]
        xb_ref[...] = x.astype(BF16)
        gates_ref[...] = _router_gates(x, router_ref[...])
        acc_ref[...] = jnp.zeros_like(acc_ref)

    xb = xb_ref[...]
    h = _silu(_dot(xb, wg_ref[0])) * _dot(xb, wu_ref[0])
    part = _dot(h.astype(BF16), wd_ref[0])
    lane = lax.broadcasted_iota(jnp.int32, gates_ref.shape, 1)
    gate_e = jnp.sum(jnp.where(lane == e, gates_ref[...], 0.0), axis=-1, keepdims=True)
    acc_ref[...] += gate_e * part

    @pl.when((e == pl.num_programs(1) - 1) & (f == pl.num_programs(2) - 1))
    def _():
        out_ref[...] = _layer_norm(DN_ALPHA * x_ref[...] + acc_ref[...], lng_ref[...], lnb_ref[...])


def _moe_ffn(x2, router, w_gate, w_up, w_down, ln_g, ln_b, tile, ft):
    n, d = x2.shape
    ne, _, f = w_gate.shape
    tok = pl.BlockSpec((tile, d), lambda i, e, k: (i, 0))
    return pl.pallas_call(
        _moe_dense_kernel,
        grid=(n // tile, ne, f // ft),
        in_specs=[
            tok,
            _const_spec((d, LANES)),
            pl.BlockSpec((1, d, ft), lambda i, e, k: (e, 0, k)),
            pl.BlockSpec((1, d, ft), lambda i, e, k: (e, 0, k)),
            pl.BlockSpec((1, ft, d), lambda i, e, k: (e, k, 0)),
            _const_spec((1, d)), _const_spec((1, d)),
        ],
        out_specs=tok,
        out_shape=jax.ShapeDtypeStruct((n, d), F32),
        scratch_shapes=[
            pltpu.VMEM((tile, d), BF16),
            pltpu.VMEM((tile, LANES), F32),
            pltpu.VMEM((tile, d), F32),
        ],
        compiler_params=pltpu.CompilerParams(
            dimension_semantics=("arbitrary", "arbitrary", "arbitrary"), vmem_limit_bytes=VMEM_LIMIT_BYTES),
        name="moe_ffn",
    )(x2, router, w_gate, w_up, w_down, ln_g, ln_b)


def _layer_weights(layer, w_in, conv_w, conv_b, dt_bias, a_log, d_skip, ssd_norm_w, ret_gn_w, ret_gn_b,
                   w_ret_br, w_ssd_br, w_out, ln1_g, ln1_b):
    d = w_in.shape[1]
    parts, start = [], 0
    for width in IN_SPLITS:
        parts.append(w_in[layer, :, start:start + width])
        start += width
    wq, wk, wv, wgt, wz, wxbc, wdt, wgr, wgs = parts
    half = RET_DK // 2
    perm = jnp.concatenate([jnp.arange(half) * 2, jnp.arange(half) * 2 + 1])
    wq = wq.reshape(d, RET_HEADS, RET_DK)[:, :, perm]
    wk = wk.reshape(d, RET_HEADS, RET_DK)[:, :, perm]
    wv = wv.reshape(d, RET_HEADS, RET_DV)
    wgt = wgt.reshape(d, RET_HEADS, RET_DV)
    w_ret = jnp.transpose(jnp.concatenate([wq, wk, wv, wgt], axis=2), (1, 0, 2)).astype(BF16)
    pad_h = LANES - SSD_HEADS
    return {
        "w_ret": w_ret,
        "w_gate_ret": wgr.astype(BF16),
        "w_ret_br": w_ret_br[layer].reshape(RET_HEADS, RET_DV, d).astype(BF16),
        "ret_gn_w": ret_gn_w[layer].reshape(RET_HEADS, 1, RET_DV),
        "ret_gn_b": ret_gn_b[layer].reshape(RET_HEADS, 1, RET_DV),
        "w_z": wz.astype(BF16),
        "w_xbc": wxbc.astype(BF16),
        "w_dt": jnp.pad(wdt, ((0, 0), (0, pad_h))).astype(BF16),
        "w_gate_ssd": wgs.astype(BF16),
        "w_ssd_br": w_ssd_br[layer].astype(BF16),
        "w_out": w_out[layer].astype(BF16),
        "conv_w": conv_w[layer],
        "conv_b": conv_b[layer].reshape(1, SSD_CONV_DIM),
        "dt_bias": jnp.pad(dt_bias[layer], (0, pad_h)).reshape(1, LANES),
        "a_log": jnp.pad(a_log[layer], (0, pad_h)).reshape(1, LANES),
        "d_skip": jnp.repeat(d_skip[layer], SSD_HEAD_DIM).reshape(1, SSD_INNER),
        "ssd_norm_w": ssd_norm_w[layer].reshape(1, SSD_INNER),
        "ln1_g": ln1_g[layer].reshape(1, d),
        "ln1_b": ln1_b[layer].reshape(1, d),
    }


def _forward(x, positions, w_in, conv_w, conv_b, dt_bias, a_log, d_skip, ssd_norm_w, ret_gn_w, ret_gn_b,
             w_ret_br, w_ssd_br, w_out, ln1_g, ln1_b, ln2_g, ln2_b, dense_w_gate, dense_w_up, dense_w_down,
             moe_router, moe_w_gate, moe_w_up, moe_w_down, *, ret_tile, ret_chunk, ssd_tile, ssd_chunk,
             ffn_tile, moe_ft):
    b, s, d = x.shape
    pos3 = positions.reshape(b, s, 1)
    for layer in range(w_in.shape[0]):
        lw = _layer_weights(layer, w_in, conv_w, conv_b, dt_bias, a_log, d_skip, ssd_norm_w, ret_gn_w,
                            ret_gn_b, w_ret_br, w_ssd_br, w_out, ln1_g, ln1_b)
        yret = _retention_branch(x, pos3, lw, ret_tile, ret_chunk)
        x = _ssd_branch(x, yret, lw, ssd_tile, ssd_chunk)
        x2 = x.reshape(b * s, d)
        g2 = ln2_g[layer].reshape(1, d)
        b2 = ln2_b[layer].reshape(1, d)
        i = layer // 2
        if layer % 2 == 0:
            x2 = _dense_ffn(x2, dense_w_gate[i].astype(BF16), dense_w_up[i].astype(BF16),
                            dense_w_down[i].astype(BF16), g2, b2, ffn_tile)
        else:
            router = jnp.pad(moe_router[i], ((0, 0), (0, LANES - N_EXPERTS)))
            x2 = _moe_ffn(x2, router, moe_w_gate[i].astype(BF16), moe_w_up[i].astype(BF16),
                          moe_w_down[i].astype(BF16), g2, b2, ffn_tile, moe_ft)
        x = x2.reshape(b, s, d)
    return x


def kernel(x, positions, w_in, conv_w, conv_b, dt_bias, a_log, d_skip, ssd_norm_w, ret_gn_w, ret_gn_b, w_ret_br, w_ssd_br, w_out, ln1_g, ln1_b, ln2_g, ln2_b, dense_w_gate, dense_w_up, dense_w_down, moe_router, moe_w_gate, moe_w_up, moe_w_down):
    return _forward(x, positions, w_in, conv_w, conv_b, dt_bias, a_log, d_skip, ssd_norm_w, ret_gn_w, ret_gn_b,
                    w_ret_br, w_ssd_br, w_out, ln1_g, ln1_b, ln2_g, ln2_b, dense_w_gate, dense_w_up,
                    dense_w_down, moe_router, moe_w_gate, moe_w_up, moe_w_down,
                    ret_tile=RET_TILE, ret_chunk=RET_CHUNK, ssd_tile=SSD_TILE, ssd_chunk=SSD_CHUNK,
                    ffn_tile=FFN_TILE, moe_ft=MOE_FT)
```

```python
import functools
import math

import jax
import jax.numpy as jnp
from jax import lax
from jax.experimental import pallas as pl
from jax.experimental.pallas import tpu as pltpu

F32 = jnp.float32
BF16 = jnp.bfloat16

D_MODEL = 1024
DEPTH = 4

RET_HEADS = 4
RET_DK = 256
RET_DV = 512
RET_QK = RET_HEADS * RET_DK
RET_V = RET_HEADS * RET_DV
RET_HEAD_COLS = 2 * RET_DK + 2 * RET_DV
ROPE_THETA = 10000.0

SSD_INNER = 2048
SSD_HEAD_DIM = 64
SSD_HEADS = 32
SSD_GROUPS = 8
SSD_STATE = 128
SSD_CONV = 4
SSD_CONV_DIM = SSD_INNER + 2 * SSD_GROUPS * SSD_STATE
SSD_PAIRS = SSD_HEADS // 2
SSD_GROUP_COLS = SSD_INNER // SSD_GROUPS

IN_SPLITS = (RET_QK, RET_QK, RET_V, RET_V, SSD_INNER, SSD_CONV_DIM, SSD_HEADS, D_MODEL, D_MODEL)

N_EXPERTS = 8
DN_ALPHA = (2 * DEPTH) ** 0.25
LN_EPS = 1e-5
GN_EPS = 1e-5
RMS_EPS = 1e-5

LANES = 128
CONV_HALO = 8
VMEM_LIMIT_BYTES = 56 * 1024 * 1024

RET_TILE = 512
RET_CHUNK = 256
SSD_TILE = 256
SSD_CHUNK = 128
FFN_TILE = 512
MOE_ROW_TILE = 1024
MOE_FT = 896


def _dot(a, b):
    return jnp.dot(a, b, preferred_element_type=F32)


def _dot_nt(a, b):
    return lax.dot_general(a, b, (((1,), (1,)), ((), ())), preferred_element_type=F32)


def _dot_tn(a, b):
    return lax.dot_general(a, b, (((0,), (0,)), ((), ())), preferred_element_type=F32)


def _sigmoid(v):
    return 1.0 / (1.0 + jnp.exp(-v))


def _silu(v):
    return v * _sigmoid(v)


def _layer_norm(v, g, b):
    mu = jnp.mean(v, axis=-1, keepdims=True)
    d = v - mu
    var = jnp.mean(d * d, axis=-1, keepdims=True)
    return d * lax.rsqrt(var + LN_EPS) * g + b


def _split3(v):
    hi = v.astype(BF16)
    r1 = v - hi.astype(F32)
    mid = r1.astype(BF16)
    lo = (r1 - mid.astype(F32)).astype(BF16)
    return hi, mid, lo


def _const_spec(shape):
    nd = len(shape)
    return pl.BlockSpec(shape, lambda *_: (0,) * nd, pipeline_mode=pl.Buffered(1))


def _ret_kernel(x_ref, pos_ref, invf_ref, w_ref, wg_ref, wbr_ref, gnw_ref, gnb_ref,
                intra_ref, qdec_ref, kdec_ref, cdec_ref, out_ref,
                state_ref, xb_ref, cs_ref, proj_ref, ret_ref, acc_ref, *, tile, chunk):
    half = RET_DK // 2

    @pl.when(pl.program_id(1) == 0)
    def _():
        state_ref[...] = jnp.zeros_like(state_ref)

    xb_ref[...] = x_ref[0].astype(BF16)
    ang = pos_ref[0].astype(F32) * invf_ref[...]
    cs_ref[0] = jnp.cos(ang)
    cs_ref[1] = jnp.sin(ang)
    acc_ref[...] = jnp.zeros_like(acc_ref)

    def head(h, carry):
        proj_ref[...] = _dot(xb_ref[...], w_ref[h])
        for ci in range(tile // chunk):
            rows = slice(ci * chunk, (ci + 1) * chunk)
            cos = cs_ref[0, rows, :]
            sin = cs_ref[1, rows, :]
            q1 = proj_ref[rows, 0:half]
            q2 = proj_ref[rows, half:RET_DK]
            k1 = proj_ref[rows, RET_DK:RET_DK + half]
            k2 = proj_ref[rows, RET_DK + half:2 * RET_DK]
            v = proj_ref[rows, 2 * RET_DK:2 * RET_DK + RET_DV].astype(BF16)
            g = proj_ref[rows, 2 * RET_DK + RET_DV:RET_HEAD_COLS]
            q = jnp.concatenate([q1 * cos - q2 * sin, q1 * sin + q2 * cos], axis=1)
            k = jnp.concatenate([k1 * cos - k2 * sin, k1 * sin + k2 * cos], axis=1) * (RET_DK ** -0.5)
            scores = _dot_nt(q.astype(BF16), k.astype(BF16)) * intra_ref[h]
            inner = _dot(scores.astype(BF16), v)
            st = state_ref[h]
            cross = _dot((q * qdec_ref[h]).astype(BF16), st.astype(BF16))
            state_ref[h] = st * cdec_ref[h] + _dot_tn((k * kdec_ref[h]).astype(BF16), v)
            o = inner + cross
            mu = jnp.mean(o, axis=-1, keepdims=True)
            d = o - mu
            var = jnp.mean(d * d, axis=-1, keepdims=True)
            o = d * lax.rsqrt(var + GN_EPS) * gnw_ref[h] + gnb_ref[h]
            ret_ref[rows, :] = (_silu(g) * o).astype(BF16)
        acc_ref[...] += _dot(ret_ref[...], wbr_ref[h])
        return carry

    lax.fori_loop(0, RET_HEADS, head, 0)
    gate = _sigmoid(_dot(xb_ref[...], wg_ref[...]))
    out_ref[0] = gate * acc_ref[...]


def _ret_tables(chunk):
    log_gamma = jnp.log(1.0 - 2.0 ** (-5.0 - jnp.arange(RET_HEADS, dtype=F32)))
    pos = jnp.arange(chunk, dtype=F32)
    diff = pos[:, None] - pos[None, :]
    causal = diff >= 0
    intra = jnp.where(causal[None], jnp.exp(log_gamma[:, None, None] * jnp.where(causal, diff, 0.0)[None]), 0.0)
    qdec = jnp.exp(log_gamma[:, None] * (pos[None, :] + 1.0))
    kdec = jnp.exp(log_gamma[:, None] * (chunk - 1.0 - pos[None, :]))
    cdec = jnp.exp(log_gamma * chunk)
    qdec = jnp.broadcast_to(qdec[:, :, None], (RET_HEADS, chunk, RET_DK))
    kdec = jnp.broadcast_to(kdec[:, :, None], (RET_HEADS, chunk, RET_DK))
    cdec = jnp.broadcast_to(cdec[:, None, None], (RET_HEADS, 1, RET_DV))
    return intra, qdec, kdec, cdec


def _retention_branch(x, pos3, lw, tile, chunk):
    b, s, d = x.shape
    half = RET_DK // 2
    inv_freq = (ROPE_THETA ** (-jnp.arange(half, dtype=F32) / half)).reshape(1, half)
    intra, qdec, kdec, cdec = _ret_tables(chunk)
    kern = functools.partial(_ret_kernel, tile=tile, chunk=chunk)
    return pl.pallas_call(
        kern,
        grid=(b, s // tile),
        in_specs=[
            pl.BlockSpec((1, tile, d), lambda i, j: (i, j, 0)),
            pl.BlockSpec((1, tile, 1), lambda i, j: (i, j, 0)),
            _const_spec((1, half)),
            _const_spec((RET_HEADS, d, RET_HEAD_COLS)),
            _const_spec((d, d)),
            _const_spec((RET_HEADS, RET_DV, d)),
            _const_spec((RET_HEADS, 1, RET_DV)),
            _const_spec((RET_HEADS, 1, RET_DV)),
            _const_spec((RET_HEADS, chunk, chunk)),
            _const_spec((RET_HEADS, chunk, RET_DK)),
            _const_spec((RET_HEADS, chunk, RET_DK)),
            _const_spec((RET_HEADS, 1, RET_DV)),
        ],
        out_specs=pl.BlockSpec((1, tile, d), lambda i, j: (i, j, 0)),
        out_shape=jax.ShapeDtypeStruct((b, s, d), F32),
        scratch_shapes=[
            pltpu.VMEM((RET_HEADS, RET_DK, RET_DV), F32),
            pltpu.VMEM((tile, d), BF16),
            pltpu.VMEM((2, tile, half), F32),
            pltpu.VMEM((tile, RET_HEAD_COLS), F32),
            pltpu.VMEM((tile, RET_DV), BF16),
            pltpu.VMEM((tile, d), F32),
        ],
        compiler_params=pltpu.CompilerParams(
            dimension_semantics=("arbitrary", "arbitrary"), vmem_limit_bytes=VMEM_LIMIT_BYTES),
        name="retention_branch",
    )(x, pos3, inv_freq, lw["w_ret"], lw["w_gate_ret"], lw["w_ret_br"], lw["ret_gn_w"], lw["ret_gn_b"],
      intra, qdec, kdec, cdec)


def _ssd_kernel(x_ref, yret_ref, wz_ref, wxbc_ref, wdt_ref, wg_ref, wbr_ref, wout_ref,
                convw_ref, convb_ref, dtb_ref, alog_ref, dskip_ref, normw_ref, lng_ref, lnb_ref,
                out_ref, raw_ref, xbc_ref, z_ref, dt_ref, y_ref, state_ref, *, tile, chunk):
    @pl.when(pl.program_id(1) == 0)
    def _():
        raw_ref[0:CONV_HALO, :] = jnp.zeros((CONV_HALO, SSD_CONV_DIM), F32)
        state_ref[...] = jnp.zeros_like(state_ref)

    xb = x_ref[0].astype(BF16)
    raw_ref[CONV_HALO:CONV_HALO + tile, :] = _dot(xb, wxbc_ref[...])
    z_ref[...] = _dot(xb, wz_ref[...])
    dtr = _dot(xb, wdt_ref[...]) + dtb_ref[...]
    dt_ref[...] = jnp.maximum(dtr, 0.0) + jnp.log1p(jnp.exp(-jnp.abs(dtr)))

    conv = convb_ref[...]
    for k in range(SSD_CONV):
        off = CONV_HALO - (SSD_CONV - 1) + k
        conv = conv + convw_ref[k:k + 1, :] * raw_ref[off:off + tile, :]
    xbc_ref[...] = _silu(conv)
    raw_ref[0:CONV_HALO, :] = raw_ref[tile:tile + CONV_HALO, :]

    a_neg = -jnp.exp(alog_ref[...])
    rows_i = lax.broadcasted_iota(jnp.int32, (chunk, chunk), 0)
    cols_i = lax.broadcasted_iota(jnp.int32, (chunk, chunk), 1)
    tri = rows_i >= cols_i
    tril_bf = tri.astype(BF16)
    low_half = lax.broadcasted_iota(jnp.int32, (1, LANES), 1) < SSD_HEAD_DIM
    b_off = SSD_INNER
    c_off = SSD_INNER + SSD_GROUPS * SSD_STATE

    def chunk_body(ci, carry):
        r0 = pl.multiple_of(ci * chunk, chunk)
        rows = pl.ds(r0, chunk)
        dt_c = dt_ref[rows, :]
        hi, mid, lo = _split3(dt_c * a_neg)
        a_cs = _dot(tril_bf, hi) + _dot(tril_bf, mid) + _dot(tril_bf, lo)
        a_row = a_cs.T
        dt_row = dt_c.T
        wdt_row = jnp.exp(a_row[:, chunk - 1:chunk] - a_row) * dt_row
        for g in range(SSD_GROUPS):
            bc = xbc_ref[rows, b_off + g * SSD_STATE:b_off + (g + 1) * SSD_STATE]
            cc = xbc_ref[rows, c_off + g * SSD_STATE:c_off + (g + 1) * SSD_STATE]
            cbm = jnp.where(tri, _dot_nt(cc.astype(BF16), bc.astype(BF16)), 0.0)
            bct = bc.T
            ys = []
            for p in range(2):
                pair = 2 * g + p
                hd0 = 2 * pair
                hd1 = hd0 + 1
                xs_p = xbc_ref[rows, pair * LANES:(pair + 1) * LANES]
                col0 = jnp.broadcast_to(a_cs[:, hd0:hd0 + 1], (chunk, LANES))
                col1 = jnp.broadcast_to(a_cs[:, hd1:hd1 + 1], (chunk, LANES))
                l0 = cbm * jnp.exp(jnp.minimum(col0 - a_row[hd0:hd0 + 1, :], 0.0)) * dt_row[hd0:hd0 + 1, :]
                l1 = cbm * jnp.exp(jnp.minimum(col1 - a_row[hd1:hd1 + 1, :], 0.0)) * dt_row[hd1:hd1 + 1, :]
                e0 = jnp.exp(col0)
                e1 = jnp.exp(col1)
                lhs_y = jnp.concatenate([l0, l1, cc * e0, cc * e1], axis=1).astype(BF16)
                st = state_ref[pair]
                xs_lo = jnp.where(low_half, xs_p, 0.0)
                xs_hi = jnp.where(low_half, 0.0, xs_p)
                st_lo = jnp.where(low_half, st, 0.0)
                st_hi = jnp.where(low_half, 0.0, st)
                rhs = jnp.concatenate([xs_lo, xs_hi, st_lo, st_hi], axis=0).astype(BF16)
                y = _dot(lhs_y, rhs) + dskip_ref[:, pair * LANES:(pair + 1) * LANES] * xs_p
                lhs_s = jnp.concatenate(
                    [bct * wdt_row[hd0:hd0 + 1, :], bct * wdt_row[hd1:hd1 + 1, :]], axis=1).astype(BF16)
                e_last = jnp.where(low_half, e0[chunk - 1:chunk, :], e1[chunk - 1:chunk, :])
                state_ref[pair] = st * e_last + _dot(lhs_s, rhs[0:2 * chunk, :])
                ys.append(y)
            gcols = slice(g * SSD_GROUP_COLS, (g + 1) * SSD_GROUP_COLS)
            yg = jnp.concatenate(ys, axis=1) * _silu(z_ref[rows, gcols])
            ms = jnp.mean(yg * yg, axis=-1, keepdims=True)
            y_ref[rows, gcols] = (yg * lax.rsqrt(ms + RMS_EPS) * normw_ref[:, gcols]).astype(BF16)
        return carry

    lax.fori_loop(0, tile // chunk, chunk_body, 0)

    y_ssd = _dot(y_ref[...], wbr_ref[...])
    merged = _sigmoid(_dot(xb, wg_ref[...])) * y_ssd + yret_ref[0]
    mixed = _dot(merged.astype(BF16), wout_ref[...])
    out_ref[0] = _layer_norm(DN_ALPHA * x_ref[0] + mixed, lng_ref[...], lnb_ref[...])


def _ssd_branch(x, yret, lw, tile, chunk):
    b, s, d = x.shape
    kern = functools.partial(_ssd_kernel, tile=tile, chunk=chunk)
    tok = pl.BlockSpec((1, tile, d), lambda i, j: (i, j, 0))
    return pl.pallas_call(
        kern,
        grid=(b, s // tile),
        in_specs=[
            tok, tok,
            _const_spec((d, SSD_INNER)),
            _const_spec((d, SSD_CONV_DIM)),
            _const_spec((d, LANES)),
            _const_spec((d, d)),
            _const_spec((SSD_INNER, d)),
            _const_spec((d, d)),
            _const_spec((SSD_CONV, SSD_CONV_DIM)),
            _const_spec((1, SSD_CONV_DIM)),
            _const_spec((1, LANES)),
            _const_spec((1, LANES)),
            _const_spec((1, SSD_INNER)),
            _const_spec((1, SSD_INNER)),
            _const_spec((1, d)),
            _const_spec((1, d)),
        ],
        out_specs=tok,
        out_shape=jax.ShapeDtypeStruct((b, s, d), F32),
        scratch_shapes=[
            pltpu.VMEM((tile + CONV_HALO, SSD_CONV_DIM), F32),
            pltpu.VMEM((tile, SSD_CONV_DIM), F32),
            pltpu.VMEM((tile, SSD_INNER), F32),
            pltpu.VMEM((tile, LANES), F32),
            pltpu.VMEM((tile, SSD_INNER), BF16),
            pltpu.VMEM((SSD_PAIRS, SSD_STATE, LANES), F32),
        ],
        compiler_params=pltpu.CompilerParams(
            dimension_semantics=("arbitrary", "arbitrary"), vmem_limit_bytes=VMEM_LIMIT_BYTES),
        name="ssd_branch",
    )(x, yret, lw["w_z"], lw["w_xbc"], lw["w_dt"], lw["w_gate_ssd"], lw["w_ssd_br"], lw["w_out"],
      lw["conv_w"], lw["conv_b"], lw["dt_bias"], lw["a_log"], lw["d_skip"], lw["ssd_norm_w"],
      lw["ln1_g"], lw["ln1_b"])


def _dense_ffn_kernel(x_ref, wg_ref, wu_ref, wd_ref, lng_ref, lnb_ref, out_ref):
    x = x_ref[...]
    xb = x.astype(BF16)
    h = _silu(_dot(xb, wg_ref[...])) * _dot(xb, wu_ref[...])
    f = _dot(h.astype(BF16), wd_ref[...])
    out_ref[...] = _layer_norm(DN_ALPHA * x + f, lng_ref[...], lnb_ref[...])


def _dense_ffn(x2, w_gate, w_up, w_down, ln_g, ln_b, tile):
    n, d = x2.shape
    f = w_gate.shape[1]
    tok = pl.BlockSpec((tile, d), lambda i: (i, 0))
    return pl.pallas_call(
        _dense_ffn_kernel,
        grid=(n // tile,),
        in_specs=[tok, _const_spec((d, f)), _const_spec((d, f)), _const_spec((f, d)),
                  _const_spec((1, d)), _const_spec((1, d))],
        out_specs=tok,
        out_shape=jax.ShapeDtypeStruct((n, d), F32),
        compiler_params=pltpu.CompilerParams(
            dimension_semantics=("arbitrary",), vmem_limit_bytes=VMEM_LIMIT_BYTES),
        name="dense_ffn",
    )(x2, w_gate, w_up, w_down, ln_g, ln_b)


SLAB = 8


def _slab_copy(src_ref, src_row, dst_ref, dst_row, sem):
    src = src_ref.at[pl.ds(pl.multiple_of(src_row * SLAB, SLAB), SLAB), :]
    dst = dst_ref.at[pl.ds(pl.multiple_of(dst_row * SLAB, SLAB), SLAB), :]
    return pltpu.make_async_copy(src, dst, sem)


def _moe_route_kernel(x_ref, router_ref, eidx_ref, slot_ref, wts_ref, counts_ref, carry_ref, *, tile):
    @pl.when(pl.program_id(0) == 0)
    def _():
        carry_ref[...] = jnp.zeros_like(carry_ref)

    xh, xm, xl = _split3(x_ref[...])
    rh, rm, rl = _split3(router_ref[...])
    logits = (_dot_nt(rh, xh) + _dot_nt(rm, xh) + _dot_nt(rh, xm)) + (
        _dot_nt(rl, xh) + _dot_nt(rm, xm) + _dot_nt(rh, xl))
    nrow = logits.shape[0]
    row = lax.broadcasted_iota(jnp.int32, logits.shape, 0)
    neg = jnp.finfo(F32).min
    logits = jnp.where(row < N_EXPERTS, logits, neg)
    m1 = jnp.max(logits, axis=0, keepdims=True)
    i1 = jnp.min(jnp.where(logits == m1, row, nrow), axis=0, keepdims=True)
    rest = jnp.where(row == i1, neg, logits)
    m2 = jnp.max(rest, axis=0, keepdims=True)
    i2 = jnp.min(jnp.where(rest == m2, row, nrow), axis=0, keepdims=True)
    e = jnp.exp(m2 - m1)
    w1 = 1.0 / (1.0 + e)
    w2 = e / (1.0 + e)

    chosen = (row == i1) | (row == i2)
    t_from = lax.broadcasted_iota(jnp.int32, (tile, tile), 0)
    t_to = lax.broadcasted_iota(jnp.int32, (tile, tile), 1)
    before = (t_from < t_to).astype(BF16)
    slot = carry_ref[...] + _dot(chosen.astype(BF16), before)
    carry_new = carry_ref[...] + jnp.sum(chosen.astype(F32), axis=1, keepdims=True)
    carry_ref[...] = carry_new
    eidx_ref[0, 0:1, :] = i1
    eidx_ref[0, 1:2, :] = i2
    slot_ref[0, 0:1, :] = jnp.sum(jnp.where(row == i1, slot, 0.0), axis=0, keepdims=True).astype(jnp.int32)
    slot_ref[0, 1:2, :] = jnp.sum(jnp.where(row == i2, slot, 0.0), axis=0, keepdims=True).astype(jnp.int32)
    counts_ref[...] = carry_new[:, 0:LANES].astype(jnp.int32)
    wrow = lax.broadcasted_iota(jnp.int32, (LANES, tile), 0)
    wts_ref[...] = jnp.where(wrow == 0, w1, jnp.where(wrow == 1, w2, 0.0)).T


def _moe_route(x2, router_t, tile):
    n, d = x2.shape
    rows = router_t.shape[0]
    idx_spec = pl.BlockSpec((1, 2, tile), lambda i: (i, 0, 0))
    idx_shape = jax.ShapeDtypeStruct((n // tile, 2, tile), jnp.int32)
    return pl.pallas_call(
        functools.partial(_moe_route_kernel, tile=tile),
        grid=(n // tile,),
        in_specs=[pl.BlockSpec((tile, d), lambda i: (i, 0)), _const_spec((rows, d))],
        out_specs=[idx_spec, idx_spec,
                   pl.BlockSpec((tile, LANES), lambda i: (i, 0)),
                   pl.BlockSpec((rows, LANES), lambda i: (0, 0))],
        out_shape=[idx_shape, idx_shape,
                   jax.ShapeDtypeStruct((n, LANES), F32),
                   jax.ShapeDtypeStruct((rows, LANES), jnp.int32)],
        scratch_shapes=[pltpu.VMEM((rows, tile), F32)],
        compiler_params=pltpu.CompilerParams(
            dimension_semantics=("arbitrary",), vmem_limit_bytes=VMEM_LIMIT_BYTES),
        name="moe_route",
    )(x2, router_t)


def _moe_dispatch_kernel(fill_lo_ref, fill_hi_ref, x_ref, dest_ref, xs_ref,
                         slab_ref, zero_ref, dest_smem, sem_idx, sem_rows, *, tile):
    idx_copy = pltpu.make_async_copy(dest_ref.at[0], dest_smem, sem_idx)
    idx_copy.start()
    for j in range(SLAB):
        slab_ref[pl.ds(j, tile, stride=SLAB), :] = x_ref[:, j * LANES:(j + 1) * LANES]
    idx_copy.wait()

    def send(t, carry):
        _slab_copy(slab_ref, t, xs_ref, dest_smem[0, t], sem_rows).start()
        _slab_copy(slab_ref, t, xs_ref, dest_smem[1, t], sem_rows).start()
        return carry

    def drain(t, carry):
        _slab_copy(slab_ref, 0, xs_ref, 0, sem_rows).wait()
        _slab_copy(slab_ref, 0, xs_ref, 0, sem_rows).wait()
        return carry

    lax.fori_loop(0, tile, send, 0, unroll=8)
    lax.fori_loop(0, tile, drain, 0, unroll=8)

    @pl.when(pl.program_id(0) == pl.num_programs(0) - 1)
    def _():
        zero_ref[...] = jnp.zeros_like(zero_ref)
        for k in range(N_EXPERTS + 1):
            lo = fill_lo_ref[k]
            hi = fill_hi_ref[k]

            def fill(r, carry):
                _slab_copy(zero_ref, 0, xs_ref, r, sem_rows).start()
                return carry

            def fill_wait(r, carry):
                _slab_copy(zero_ref, 0, xs_ref, 0, sem_rows).wait()
                return carry

            lax.fori_loop(lo, hi, fill, 0)
            lax.fori_loop(lo, hi, fill_wait, 0)


def _moe_dispatch(x2, dest, fill_lo, fill_hi, n_rows, tile):
    n, d = x2.shape
    grid_spec = pltpu.PrefetchScalarGridSpec(
        num_scalar_prefetch=2,
        grid=(n // tile,),
        in_specs=[pl.BlockSpec((tile, d), lambda i, lo, hi: (i, 0)),
                  pl.BlockSpec((1, 2, tile), lambda i, lo, hi: (i, 0, 0))],
        out_specs=pl.BlockSpec(memory_space=pl.ANY),
        scratch_shapes=[
            pltpu.VMEM((tile * SLAB, LANES), F32),
            pltpu.VMEM((SLAB, LANES), F32),
            pltpu.SMEM((2, tile), jnp.int32),
            pltpu.SemaphoreType.DMA,
            pltpu.SemaphoreType.DMA,
        ],
    )
    return pl.pallas_call(
        functools.partial(_moe_dispatch_kernel, tile=tile),
        grid_spec=grid_spec,
        out_shape=jax.ShapeDtypeStruct((n_rows * SLAB, LANES), F32),
        compiler_params=pltpu.CompilerParams(
            dimension_semantics=("arbitrary",), vmem_limit_bytes=VMEM_LIMIT_BYTES),
        name="moe_dispatch",
    )(fill_lo, fill_hi, x2, dest)


def _moe_expert_kernel(blk_ref, exp_ref, valid_ref, xs_ref, wg_ref, wu_ref, wd_ref, ys_ref, xb_ref, acc_ref,
                       *, row_tile):
    i = pl.program_id(0)
    f = pl.program_id(1)

    @pl.when(valid_ref[i] == 1)
    def _():
        @pl.when(f == 0)
        def _():
            for j in range(SLAB):
                xb_ref[:, j * LANES:(j + 1) * LANES] = xs_ref[pl.ds(j, row_tile, stride=SLAB), :].astype(BF16)
            acc_ref[...] = jnp.zeros_like(acc_ref)

        xb = xb_ref[...]
        h = _silu(_dot(xb, wg_ref[0])) * _dot(xb, wu_ref[0])
        acc_ref[...] += _dot(h.astype(BF16), wd_ref[0])

        @pl.when(f == pl.num_programs(1) - 1)
        def _():
            for j in range(SLAB):
                ys_ref[pl.ds(j, row_tile, stride=SLAB), :] = acc_ref[:, j * LANES:(j + 1) * LANES]

    @pl.when((valid_ref[i] == 0) & (f == 0))
    def _():
        ys_ref[...] = jnp.zeros_like(ys_ref)


def _moe_experts(xs, blk, exp, valid, w_gate, w_up, w_down, row_tile, ft):
    f = w_gate.shape[2]
    d = w_gate.shape[1]
    nf = f // ft
    n_steps = blk.shape[0]

    def f_idx(i, k, valid_ref):
        return jnp.where(valid_ref[i] == 1, k, nf - 1)

    rows_spec = pl.BlockSpec((row_tile * SLAB, LANES), lambda i, k, b, e, v: (b[i], 0))
    grid_spec = pltpu.PrefetchScalarGridSpec(
        num_scalar_prefetch=3,
        grid=(n_steps, nf),
        in_specs=[
            rows_spec,
            pl.BlockSpec((1, d, ft), lambda i, k, b, e, v: (e[i], 0, f_idx(i, k, v))),
            pl.BlockSpec((1, d, ft), lambda i, k, b, e, v: (e[i], 0, f_idx(i, k, v))),
            pl.BlockSpec((1, ft, d), lambda i, k, b, e, v: (e[i], f_idx(i, k, v), 0)),
        ],
        out_specs=rows_spec,
        scratch_shapes=[pltpu.VMEM((row_tile, d), BF16), pltpu.VMEM((row_tile, d), F32)],
    )
    return pl.pallas_call(
        functools.partial(_moe_expert_kernel, row_tile=row_tile),
        grid_spec=grid_spec,
        out_shape=jax.ShapeDtypeStruct(xs.shape, F32),
        compiler_params=pltpu.CompilerParams(
            dimension_semantics=("arbitrary", "arbitrary"), vmem_limit_bytes=VMEM_LIMIT_BYTES),
        name="moe_experts",
    )(blk, exp, valid, xs, w_gate, w_up, w_down)


def _moe_combine_kernel(x_ref, dest_ref, wts_ref, ys_ref, lng_ref, lnb_ref, out_ref,
                        y1_ref, y2_ref, dest_smem, sem_idx, sem_rows, *, tile):
    idx_copy = pltpu.make_async_copy(dest_ref.at[0], dest_smem, sem_idx)
    idx_copy.start()
    idx_copy.wait()

    def fetch(t, carry):
        _slab_copy(ys_ref, dest_smem[0, t], y1_ref, t, sem_rows).start()
        _slab_copy(ys_ref, dest_smem[1, t], y2_ref, t, sem_rows).start()
        return carry

    def drain(t, carry):
        _slab_copy(ys_ref, 0, y1_ref, 0, sem_rows).wait()
        _slab_copy(ys_ref, 0, y2_ref, 0, sem_rows).wait()
        return carry

    lax.fori_loop(0, tile, fetch, 0, unroll=8)
    lax.fori_loop(0, tile, drain, 0, unroll=8)

    w = wts_ref[...]
    w1 = w[:, 0:1]
    w2 = w[:, 1:2]
    f = jnp.concatenate(
        [w1 * y1_ref[pl.ds(j, tile, stride=SLAB), :] + w2 * y2_ref[pl.ds(j, tile, stride=SLAB), :]
         for j in range(SLAB)], axis=1)
    out_ref[...] = _layer_norm(DN_ALPHA * x_ref[...] + f, lng_ref[...], lnb_ref[...])


def _moe_combine(x2, dest, wts, ys, ln_g, ln_b, tile):
    n, d = x2.shape
    tok = pl.BlockSpec((tile, d), lambda i: (i, 0))
    return pl.pallas_call(
        functools.partial(_moe_combine_kernel, tile=tile),
        grid=(n // tile,),
        in_specs=[
            tok,
            pl.BlockSpec((1, 2, tile), lambda i: (i, 0, 0)),
            pl.BlockSpec((tile, LANES), lambda i: (i, 0)),
            pl.BlockSpec(memory_space=pl.ANY),
            _const_spec((1, d)), _const_spec((1, d)),
        ],
        out_specs=tok,
        out_shape=jax.ShapeDtypeStruct((n, d), F32),
        scratch_shapes=[
            pltpu.VMEM((tile * SLAB, LANES), F32),
            pltpu.VMEM((tile * SLAB, LANES), F32),
            pltpu.SMEM((2, tile), jnp.int32),
            pltpu.SemaphoreType.DMA,
            pltpu.SemaphoreType.DMA,
        ],
        compiler_params=pltpu.CompilerParams(
            dimension_semantics=("arbitrary",), vmem_limit_bytes=VMEM_LIMIT_BYTES),
        name="moe_combine",
    )(x2, dest, wts, ys, ln_g, ln_b)


def _moe_plan(counts, eidx, slot, n_tiles, row_tile):
    padded = ((counts + row_tile - 1) // row_tile) * row_tile
    ends = jnp.cumsum(padded)
    offs = ends - padded
    dest = offs[eidx] + slot
    used_tiles = ends[-1] // row_tile
    steps = jnp.arange(n_tiles, dtype=jnp.int32)
    valid = (steps < used_tiles).astype(jnp.int32)
    clipped = jnp.minimum(steps, used_tiles - 1)
    exp = jnp.sum((clipped[:, None] * row_tile >= ends[None, :]).astype(jnp.int32), axis=1)
    blk = steps
    fill_lo = jnp.concatenate([offs + counts, ends[-1:]])
    fill_hi = jnp.concatenate([ends, jnp.full((1,), n_tiles * row_tile, jnp.int32)])
    return (dest.astype(jnp.int32), blk, exp.astype(jnp.int32), valid,
            fill_lo.astype(jnp.int32), fill_hi.astype(jnp.int32))


def _moe_ffn(x2, router, w_gate, w_up, w_down, ln_g, ln_b, tile, row_tile, ft):
    n, d = x2.shape
    assert d == SLAB * LANES
    router_t = jnp.pad(router.T, ((0, 2 * SLAB - N_EXPERTS), (0, 0)))
    eidx, slot, wts, counts = _moe_route(x2, router_t, tile)
    n_tiles = (2 * n) // row_tile + N_EXPERTS
    dest, blk, exp, valid, fill_lo, fill_hi = _moe_plan(counts[:N_EXPERTS, 0], eidx, slot, n_tiles, row_tile)
    xs = _moe_dispatch(x2, dest, fill_lo, fill_hi, n_tiles * row_tile, tile)
    ys = _moe_experts(xs, blk, exp, valid, w_gate, w_up, w_down, row_tile, ft)
    return _moe_combine(x2, dest, wts, ys, ln_g, ln_b, tile)


def _layer_weights(layer, w_in, conv_w, conv_b, dt_bias, a_log, d_skip, ssd_norm_w, ret_gn_w, ret_gn_b,
                   w_ret_br, w_ssd_br, w_out, ln1_g, ln1_b):
    d = w_in.shape[1]
    parts, start = [], 0
    for width in IN_SPLITS:
        parts.append(w_in[layer, :, start:start + width])
        start += width
    wq, wk, wv, wgt, wz, wxbc, wdt, wgr, wgs = parts
    half = RET_DK // 2
    perm = jnp.concatenate([jnp.arange(half) * 2, jnp.arange(half) * 2 + 1])
    wq = wq.reshape(d, RET_HEADS, RET_DK)[:, :, perm]
    wk = wk.reshape(d, RET_HEADS, RET_DK)[:, :, perm]
    wv = wv.reshape(d, RET_HEADS, RET_DV)
    wgt = wgt.reshape(d, RET_HEADS, RET_DV)
    w_ret = jnp.transpose(jnp.concatenate([wq, wk, wv, wgt], axis=2), (1, 0, 2)).astype(BF16)
    pad_h = LANES - SSD_HEADS
    return {
        "w_ret": w_ret,
        "w_gate_ret": wgr.astype(BF16),
        "w_ret_br": w_ret_br[layer].reshape(RET_HEADS, RET_DV, d).astype(BF16),
        "ret_gn_w": ret_gn_w[layer].reshape(RET_HEADS, 1, RET_DV),
        "ret_gn_b": ret_gn_b[layer].reshape(RET_HEADS, 1, RET_DV),
        "w_z": wz.astype(BF16),
        "w_xbc": wxbc.astype(BF16),
        "w_dt": jnp.pad(wdt, ((0, 0), (0, pad_h))).astype(BF16),
        "w_gate_ssd": wgs.astype(BF16),
        "w_ssd_br": w_ssd_br[layer].astype(BF16),
        "w_out": w_out[layer].astype(BF16),
        "conv_w": conv_w[layer],
        "conv_b": conv_b[layer].reshape(1, SSD_CONV_DIM),
        "dt_bias": jnp.pad(dt_bias[layer], (0, pad_h)).reshape(1, LANES),
        "a_log": jnp.pad(a_log[layer], (0, pad_h)).reshape(1, LANES),
        "d_skip": jnp.repeat(d_skip[layer], SSD_HEAD_DIM).reshape(1, SSD_INNER),
        "ssd_norm_w": ssd_norm_w[layer].reshape(1, SSD_INNER),
        "ln1_g": ln1_g[layer].reshape(1, d),
        "ln1_b": ln1_b[layer].reshape(1, d),
    }


def _forward(x, positions, w_in, conv_w, conv_b, dt_bias, a_log, d_skip, ssd_norm_w, ret_gn_w, ret_gn_b,
             w_ret_br, w_ssd_br, w_out, ln1_g, ln1_b, ln2_g, ln2_b, dense_w_gate, dense_w_up, dense_w_down,
             moe_router, moe_w_gate, moe_w_up, moe_w_down, *, ret_tile, ret_chunk, ssd_tile, ssd_chunk,
             ffn_tile, moe_row_tile, moe_ft):
    b, s, d = x.shape
    pos3 = positions.reshape(b, s, 1)
    for layer in range(w_in.shape[0]):
        lw = _layer_weights(layer, w_in, conv_w, conv_b, dt_bias, a_log, d_skip, ssd_norm_w, ret_gn_w,
                            ret_gn_b, w_ret_br, w_ssd_br, w_out, ln1_g, ln1_b)
        yret = _retention_branch(x, pos3, lw, ret_tile, ret_chunk)
        x = _ssd_branch(x, yret, lw, ssd_tile, ssd_chunk)
        x2 = x.reshape(b * s, d)
        g2 = ln2_g[layer].reshape(1, d)
        b2 = ln2_b[layer].reshape(1, d)
        i = layer // 2
        if layer % 2 == 0:
            x2 = _dense_ffn(x2, dense_w_gate[i].astype(BF16), dense_w_up[i].astype(BF16),
                            dense_w_down[i].astype(BF16), g2, b2, ffn_tile)
        else:
            x2 = _moe_ffn(x2, moe_router[i], moe_w_gate[i].astype(BF16), moe_w_up[i].astype(BF16),
                          moe_w_down[i].astype(BF16), g2, b2, ffn_tile, moe_row_tile, moe_ft)
        x = x2.reshape(b, s, d)
    return x


def kernel(x, positions, w_in, conv_w, conv_b, dt_bias, a_log, d_skip, ssd_norm_w, ret_gn_w, ret_gn_b, w_ret_br, w_ssd_br, w_out, ln1_g, ln1_b, ln2_g, ln2_b, dense_w_gate, dense_w_up, dense_w_down, moe_router, moe_w_gate, moe_w_up, moe_w_down):
    return _forward(x, positions, w_in, conv_w, conv_b, dt_bias, a_log, d_skip, ssd_norm_w, ret_gn_w, ret_gn_b,
                    w_ret_br, w_ssd_br, w_out, ln1_g, ln1_b, ln2_g, ln2_b, dense_w_gate, dense_w_up,
                    dense_w_down, moe_router, moe_w_gate, moe_w_up, moe_w_down,
                    ret_tile=RET_TILE, ret_chunk=RET_CHUNK, ssd_tile=SSD_TILE, ssd_chunk=SSD_CHUNK,
                    ffn_tile=FFN_TILE, moe_row_tile=MOE_ROW_TILE, moe_ft=MOE_FT)
```

```python
import functools
import math

import jax
import jax.numpy as jnp
from jax import lax
from jax.experimental import pallas as pl
from jax.experimental.pallas import tpu as pltpu

F32 = jnp.float32
BF16 = jnp.bfloat16

D_MODEL = 1024
DEPTH = 4

RET_HEADS = 4
RET_DK = 256
RET_DV = 512
RET_QK = RET_HEADS * RET_DK
RET_V = RET_HEADS * RET_DV
RET_HEAD_COLS = 2 * RET_DK + 2 * RET_DV
ROPE_THETA = 10000.0

SSD_INNER = 2048
SSD_HEAD_DIM = 64
SSD_HEADS = 32
SSD_GROUPS = 8
SSD_STATE = 128
SSD_CONV = 4
SSD_CONV_DIM = SSD_INNER + 2 * SSD_GROUPS * SSD_STATE
SSD_PAIRS = SSD_HEADS // 2
SSD_GROUP_COLS = SSD_INNER // SSD_GROUPS
SSD_UNIT_COLS = 256

IN_SPLITS = (RET_QK, RET_QK, RET_V, RET_V, SSD_INNER, SSD_CONV_DIM, SSD_HEADS, D_MODEL, D_MODEL)

N_EXPERTS = 8
DN_ALPHA = (2 * DEPTH) ** 0.25
LN_EPS = 1e-5
GN_EPS = 1e-5
RMS_EPS = 1e-5

LANES = 128
CONV_HALO = 8
VMEM_LIMIT_BYTES = 56 * 1024 * 1024

RET_TILE = 512
RET_CHUNK = 256
SSD_TILE = 256
SSD_CHUNK = 128
FFN_TILE = 512
MOE_ROW_TILE = 1024
MOE_FT = 512


def _dot(a, b):
    return jnp.dot(a, b, preferred_element_type=F32)


def _dot_nt(a, b):
    return lax.dot_general(a, b, (((1,), (1,)), ((), ())), preferred_element_type=F32)


def _dot_tn(a, b):
    return lax.dot_general(a, b, (((0,), (0,)), ((), ())), preferred_element_type=F32)


def _sigmoid(v):
    return 1.0 / (1.0 + jnp.exp(-v))


def _silu(v):
    return v * _sigmoid(v)


def _layer_norm(v, g, b):
    mu = jnp.mean(v, axis=-1, keepdims=True)
    d = v - mu
    var = jnp.mean(d * d, axis=-1, keepdims=True)
    return d * lax.rsqrt(var + LN_EPS) * g + b


def _split3(v):
    hi = v.astype(BF16)
    r1 = v - hi.astype(F32)
    mid = r1.astype(BF16)
    lo = (r1 - mid.astype(F32)).astype(BF16)
    return hi, mid, lo


def _const_spec(shape):
    nd = len(shape)
    return pl.BlockSpec(shape, lambda *_: (0,) * nd, pipeline_mode=pl.Buffered(1))


def _ret_kernel(x_ref, pos_ref, invf_ref, w_ref, wg_ref, wbr_ref, gnw_ref, gnb_ref,
                intra_ref, qdec_ref, kdec_ref, cdec_ref, out_ref,
                state_ref, xb_ref, cs_ref, proj0_ref, proj1_ref, ret0_ref, ret1_ref, *, tile, chunk):
    half = RET_DK // 2

    @pl.when(pl.program_id(1) == 0)
    def _():
        state_ref[...] = jnp.zeros_like(state_ref)

    xb_ref[...] = x_ref[0].astype(BF16)
    ang = pos_ref[0].astype(F32) * invf_ref[...]
    cs_ref[0] = jnp.cos(ang)
    cs_ref[1] = jnp.sin(ang)
    acc = None

    for h in range(RET_HEADS):
        proj_ref = (proj0_ref, proj1_ref)[h % 2]
        ret_ref = (ret0_ref, ret1_ref)[h % 2]
        proj_ref[...] = _dot(xb_ref[...], w_ref[h])
        for ci in range(tile // chunk):
            rows = slice(ci * chunk, (ci + 1) * chunk)
            cos = cs_ref[0, rows, :]
            sin = cs_ref[1, rows, :]
            q1 = proj_ref[rows, 0:half]
            q2 = proj_ref[rows, half:RET_DK]
            k1 = proj_ref[rows, RET_DK:RET_DK + half]
            k2 = proj_ref[rows, RET_DK + half:2 * RET_DK]
            v = proj_ref[rows, 2 * RET_DK:2 * RET_DK + RET_DV].astype(BF16)
            g = proj_ref[rows, 2 * RET_DK + RET_DV:RET_HEAD_COLS]
            q = jnp.concatenate([q1 * cos - q2 * sin, q1 * sin + q2 * cos], axis=1)
            k = jnp.concatenate([k1 * cos - k2 * sin, k1 * sin + k2 * cos], axis=1) * (RET_DK ** -0.5)
            scores = _dot_nt(q.astype(BF16), k.astype(BF16)) * intra_ref[h]
            inner = _dot(scores.astype(BF16), v)
            st = state_ref[h]
            cross = _dot((q * qdec_ref[h]).astype(BF16), st.astype(BF16))
            state_ref[h] = st * cdec_ref[h] + _dot_tn((k * kdec_ref[h]).astype(BF16), v)
            o = inner + cross
            mu = jnp.mean(o, axis=-1, keepdims=True)
            d = o - mu
            var = jnp.mean(d * d, axis=-1, keepdims=True)
            o = d * lax.rsqrt(var + GN_EPS) * gnw_ref[h] + gnb_ref[h]
            ret_ref[rows, :] = (_silu(g) * o).astype(BF16)
        part = _dot(ret_ref[...], wbr_ref[h])
        acc = part if acc is None else acc + part

    gate = _sigmoid(_dot(xb_ref[...], wg_ref[...]))
    out_ref[0] = gate * acc


def _ret_tables(chunk):
    log_gamma = jnp.log(1.0 - 2.0 ** (-5.0 - jnp.arange(RET_HEADS, dtype=F32)))
    pos = jnp.arange(chunk, dtype=F32)
    diff = pos[:, None] - pos[None, :]
    causal = diff >= 0
    intra = jnp.where(causal[None], jnp.exp(log_gamma[:, None, None] * jnp.where(causal, diff, 0.0)[None]), 0.0)
    qdec = jnp.exp(log_gamma[:, None] * (pos[None, :] + 1.0))
    kdec = jnp.exp(log_gamma[:, None] * (chunk - 1.0 - pos[None, :]))
    cdec = jnp.exp(log_gamma * chunk)
    qdec = jnp.broadcast_to(qdec[:, :, None], (RET_HEADS, chunk, RET_DK))
    kdec = jnp.broadcast_to(kdec[:, :, None], (RET_HEADS, chunk, RET_DK))
    cdec = jnp.broadcast_to(cdec[:, None, None], (RET_HEADS, 1, RET_DV))
    return intra, qdec, kdec, cdec


def _retention_branch(x, pos3, lw, tile, chunk):
    b, s, d = x.shape
    half = RET_DK // 2
    inv_freq = (ROPE_THETA ** (-jnp.arange(half, dtype=F32) / half)).reshape(1, half)
    intra, qdec, kdec, cdec = _ret_tables(chunk)
    kern = functools.partial(_ret_kernel, tile=tile, chunk=chunk)
    return pl.pallas_call(
        kern,
        grid=(b, s // tile),
        in_specs=[
            pl.BlockSpec((1, tile, d), lambda i, j: (i, j, 0)),
            pl.BlockSpec((1, tile, 1), lambda i, j: (i, j, 0)),
            _const_spec((1, half)),
            _const_spec((RET_HEADS, d, RET_HEAD_COLS)),
            _const_spec((d, d)),
            _const_spec((RET_HEADS, RET_DV, d)),
            _const_spec((RET_HEADS, 1, RET_DV)),
            _const_spec((RET_HEADS, 1, RET_DV)),
            _const_spec((RET_HEADS, chunk, chunk)),
            _const_spec((RET_HEADS, chunk, RET_DK)),
            _const_spec((RET_HEADS, chunk, RET_DK)),
            _const_spec((RET_HEADS, 1, RET_DV)),
        ],
        out_specs=pl.BlockSpec((1, tile, d), lambda i, j: (i, j, 0)),
        out_shape=jax.ShapeDtypeStruct((b, s, d), F32),
        scratch_shapes=[
            pltpu.VMEM((RET_HEADS, RET_DK, RET_DV), F32),
            pltpu.VMEM((tile, d), BF16),
            pltpu.VMEM((2, tile, half), F32),
            pltpu.VMEM((tile, RET_HEAD_COLS), F32),
            pltpu.VMEM((tile, RET_HEAD_COLS), F32),
            pltpu.VMEM((tile, RET_DV), BF16),
            pltpu.VMEM((tile, RET_DV), BF16),
        ],
        compiler_params=pltpu.CompilerParams(
            dimension_semantics=("arbitrary", "arbitrary"), vmem_limit_bytes=VMEM_LIMIT_BYTES),
        name="retention_branch",
    )(x, pos3, inv_freq, lw["w_ret"], lw["w_gate_ret"], lw["w_ret_br"], lw["ret_gn_w"], lw["ret_gn_b"],
      intra, qdec, kdec, cdec)


def _col_blocks(width):
    return [slice(c, c + SSD_UNIT_COLS) for c in range(0, width, SSD_UNIT_COLS)]


def _ssd_project_units(xb_ref, wz_ref, wxbc_ref, wdt_ref, wg_ref, convw_ref, convb_ref, dtb_ref,
                       raw_ref, xbc_ref, z_ref, dt_ref, gate_ref, *, tile):
    def xbc_unit(cols):
        def run():
            raw_ref[CONV_HALO:CONV_HALO + tile, cols] = _dot(xb_ref[...], wxbc_ref[:, cols])
        return run

    def conv_unit(cols):
        def run():
            conv = convb_ref[:, cols]
            for k in range(SSD_CONV):
                off = CONV_HALO - (SSD_CONV - 1) + k
                conv = conv + convw_ref[k:k + 1, cols] * raw_ref[off:off + tile, cols]
            xbc_ref[:, cols] = _silu(conv)
            raw_ref[0:CONV_HALO, cols] = raw_ref[tile:tile + CONV_HALO, cols]
        return run

    def z_unit(cols):
        def run():
            z_ref[:, cols] = _dot(xb_ref[...], wz_ref[:, cols])
        return run

    def gate_unit(cols):
        def run():
            gate_ref[:, cols] = _sigmoid(_dot(xb_ref[...], wg_ref[:, cols]))
        return run

    def dt_unit():
        dtr = _dot(xb_ref[...], wdt_ref[...]) + dtb_ref[...]
        dt_ref[...] = jnp.maximum(dtr, 0.0) + jnp.log1p(jnp.exp(-jnp.abs(dtr)))

    matmul_units = ([xbc_unit(c) for c in _col_blocks(SSD_CONV_DIM)] + [z_unit(c) for c in _col_blocks(SSD_INNER)]
                    + [gate_unit(c) for c in _col_blocks(D_MODEL)] + [dt_unit])
    conv_units = [conv_unit(c) for c in _col_blocks(SSD_CONV_DIM)]
    return matmul_units, conv_units


def _ssd_scan(x_ref, yret_ref, xbc_ref, z_ref, dt_ref, gate_ref, wbr_ref, wout_ref, alog_ref, dskip_ref,
              normw_ref, lng_ref, lnb_ref, out_ref, y_ref, state_ref, side_units, tail_units, *, tile, chunk):
    side_units = list(side_units)
    tail_units = list(tail_units)
    a_neg = -jnp.exp(alog_ref[...])
    rows_i = lax.broadcasted_iota(jnp.int32, (chunk, chunk), 0)
    cols_i = lax.broadcasted_iota(jnp.int32, (chunk, chunk), 1)
    tri = rows_i >= cols_i
    tril_bf = tri.astype(BF16)
    low_half = lax.broadcasted_iota(jnp.int32, (1, LANES), 1) < SSD_HEAD_DIM
    b_off = SSD_INNER
    c_off = SSD_INNER + SSD_GROUPS * SSD_STATE

    for ci in range(tile // chunk):
        rows = slice(ci * chunk, (ci + 1) * chunk)
        dt_c = dt_ref[rows, :]
        hi, mid, lo = _split3(dt_c * a_neg)
        a_cs = _dot(tril_bf, hi) + _dot(tril_bf, mid) + _dot(tril_bf, lo)
        a_row = a_cs.T
        dt_row = dt_c.T
        wdt_row = jnp.exp(a_row[:, chunk - 1:chunk] - a_row) * dt_row
        for g in range(SSD_GROUPS):
            bc = xbc_ref[rows, b_off + g * SSD_STATE:b_off + (g + 1) * SSD_STATE]
            cc = xbc_ref[rows, c_off + g * SSD_STATE:c_off + (g + 1) * SSD_STATE]
            cbm = jnp.where(tri, _dot_nt(cc.astype(BF16), bc.astype(BF16)), 0.0)
            bct = bc.T
            ys = []
            for p in range(2):
                pair = 2 * g + p
                hd0 = 2 * pair
                hd1 = hd0 + 1
                xs_p = xbc_ref[rows, pair * LANES:(pair + 1) * LANES]
                col0 = jnp.broadcast_to(a_cs[:, hd0:hd0 + 1], (chunk, LANES))
                col1 = jnp.broadcast_to(a_cs[:, hd1:hd1 + 1], (chunk, LANES))
                l0 = cbm * jnp.exp(jnp.minimum(col0 - a_row[hd0:hd0 + 1, :], 0.0)) * dt_row[hd0:hd0 + 1, :]
                l1 = cbm * jnp.exp(jnp.minimum(col1 - a_row[hd1:hd1 + 1, :], 0.0)) * dt_row[hd1:hd1 + 1, :]
                e0 = jnp.exp(col0)
                e1 = jnp.exp(col1)
                lhs_y = jnp.concatenate([l0, l1, cc * e0, cc * e1], axis=1).astype(BF16)
                st = state_ref[pair]
                xs_lo = jnp.where(low_half, xs_p, 0.0)
                xs_hi = jnp.where(low_half, 0.0, xs_p)
                st_lo = jnp.where(low_half, st, 0.0)
                st_hi = jnp.where(low_half, 0.0, st)
                rhs = jnp.concatenate([xs_lo, xs_hi, st_lo, st_hi], axis=0).astype(BF16)
                y = _dot(lhs_y, rhs) + dskip_ref[:, pair * LANES:(pair + 1) * LANES] * xs_p
                lhs_s = jnp.concatenate(
                    [bct * wdt_row[hd0:hd0 + 1, :], bct * wdt_row[hd1:hd1 + 1, :]], axis=1).astype(BF16)
                e_last = jnp.where(low_half, e0[chunk - 1:chunk, :], e1[chunk - 1:chunk, :])
                state_ref[pair] = st * e_last + _dot(lhs_s, rhs[0:2 * chunk, :])
                ys.append(y)
                if side_units:
                    side_units.pop(0)()
            gcols = slice(g * SSD_GROUP_COLS, (g + 1) * SSD_GROUP_COLS)
            yg = jnp.concatenate(ys, axis=1) * _silu(z_ref[rows, gcols])
            ms = jnp.mean(yg * yg, axis=-1, keepdims=True)
            y_ref[rows, gcols] = (yg * lax.rsqrt(ms + RMS_EPS) * normw_ref[:, gcols]).astype(BF16)

    for unit in side_units:
        unit()
    out_blocks = _col_blocks(D_MODEL)
    per_block = -(-len(tail_units) // (2 * len(out_blocks)))
    merged = []
    for c in out_blocks:
        merged.append((gate_ref[:, c] * _dot(y_ref[...], wbr_ref[:, c]) + yret_ref[0, :, c]).astype(BF16))
        for unit in tail_units[:per_block]:
            unit()
        del tail_units[:per_block]
    merged = jnp.concatenate(merged, axis=1)
    mixed = []
    for c in out_blocks:
        mixed.append(_dot(merged, wout_ref[:, c]))
        for unit in tail_units[:per_block]:
            unit()
        del tail_units[:per_block]
    for unit in tail_units:
        unit()
    out_ref[0] = _layer_norm(DN_ALPHA * x_ref[0] + jnp.concatenate(mixed, axis=1), lng_ref[...], lnb_ref[...])


def _ssd_kernel(xnext_ref, x_ref, yret_ref, wz_ref, wxbc_ref, wdt_ref, wg_ref, wbr_ref, wout_ref,
                convw_ref, convb_ref, dtb_ref, alog_ref, dskip_ref, normw_ref, lng_ref, lnb_ref,
                out_ref, raw_ref, xbc0_ref, xbc1_ref, z0_ref, z1_ref, dt0_ref, dt1_ref, gate0_ref, gate1_ref,
                y_ref, state_ref, xb_ref, *, tile, chunk):
    i = pl.program_id(0)
    j = pl.program_id(1)
    staged = ((xbc0_ref, z0_ref, dt0_ref, gate0_ref), (xbc1_ref, z1_ref, dt1_ref, gate1_ref))

    @pl.when(j == 0)
    def _():
        raw_ref[0:CONV_HALO, :] = jnp.zeros((CONV_HALO, SSD_CONV_DIM), F32)

    @pl.when((i == 0) & (j == 0))
    def _():
        for ref in staged[1]:
            ref[...] = jnp.zeros_like(ref)

    @pl.when(j <= 1)
    def _():
        state_ref[...] = jnp.zeros_like(state_ref)

    xb_ref[...] = xnext_ref[0].astype(BF16)

    for parity in range(2):
        @pl.when(lax.rem(j, 2) == parity)
        def _(parity=parity):
            matmul_units, conv_units = _ssd_project_units(
                xb_ref, wz_ref, wxbc_ref, wdt_ref, wg_ref, convw_ref, convb_ref, dtb_ref,
                raw_ref, *staged[parity], tile=tile)
            _ssd_scan(x_ref, yret_ref, *staged[1 - parity], wbr_ref, wout_ref, alog_ref, dskip_ref, normw_ref,
                      lng_ref, lnb_ref, out_ref, y_ref, state_ref, matmul_units, conv_units, tile=tile, chunk=chunk)


def _ssd_branch(x, yret, lw, tile, chunk):
    b, s, d = x.shape
    n_tiles = s // tile
    kern = functools.partial(_ssd_kernel, tile=tile, chunk=chunk)
    nxt = pl.BlockSpec((1, tile, d), lambda i, j: (i, jnp.minimum(j, n_tiles - 1), 0))
    cur = pl.BlockSpec((1, tile, d), lambda i, j: (i, jnp.maximum(j - 1, 0), 0))
    staging = [pltpu.VMEM((tile, SSD_CONV_DIM), F32)] * 2
    staging += [pltpu.VMEM((tile, SSD_INNER), F32)] * 2
    staging += [pltpu.VMEM((tile, LANES), F32)] * 2
    staging += [pltpu.VMEM((tile, d), F32)] * 2
    return pl.pallas_call(
        kern,
        grid=(b, n_tiles + 1),
        in_specs=[
            nxt, cur, cur,
            _const_spec((d, SSD_INNER)),
            _const_spec((d, SSD_CONV_DIM)),
            _const_spec((d, LANES)),
            _const_spec((d, d)),
            _const_spec((SSD_INNER, d)),
            _const_spec((d, d)),
            _const_spec((SSD_CONV, SSD_CONV_DIM)),
            _const_spec((1, SSD_CONV_DIM)),
            _const_spec((1, LANES)),
            _const_spec((1, LANES)),
            _const_spec((1, SSD_INNER)),
            _const_spec((1, SSD_INNER)),
            _const_spec((1, d)),
            _const_spec((1, d)),
        ],
        out_specs=cur,
        out_shape=jax.ShapeDtypeStruct((b, s, d), F32),
        scratch_shapes=[
            pltpu.VMEM((tile + CONV_HALO, SSD_CONV_DIM), F32),
            *staging,
            pltpu.VMEM((tile, SSD_INNER), BF16),
            pltpu.VMEM((SSD_PAIRS, SSD_STATE, LANES), F32),
            pltpu.VMEM((tile, d), BF16),
        ],
        compiler_params=pltpu.CompilerParams(
            dimension_semantics=("arbitrary", "arbitrary"), vmem_limit_bytes=VMEM_LIMIT_BYTES),
        name="ssd_branch",
    )(x, x, yret, lw["w_z"], lw["w_xbc"], lw["w_dt"], lw["w_gate_ssd"], lw["w_ssd_br"], lw["w_out"],
      lw["conv_w"], lw["conv_b"], lw["dt_bias"], lw["a_log"], lw["d_skip"], lw["ssd_norm_w"],
      lw["ln1_g"], lw["ln1_b"])


def _dense_ffn_kernel(x_ref, wg_ref, wu_ref, wd_ref, lng_ref, lnb_ref, out_ref):
    x = x_ref[...]
    xb = x.astype(BF16)
    h = _silu(_dot(xb, wg_ref[...])) * _dot(xb, wu_ref[...])
    f = _dot(h.astype(BF16), wd_ref[...])
    out_ref[...] = _layer_norm(DN_ALPHA * x + f, lng_ref[...], lnb_ref[...])


def _dense_ffn(x2, w_gate, w_up, w_down, ln_g, ln_b, tile):
    n, d = x2.shape
    f = w_gate.shape[1]
    tok = pl.BlockSpec((tile, d), lambda i: (i, 0))
    return pl.pallas_call(
        _dense_ffn_kernel,
        grid=(n // tile,),
        in_specs=[tok, _const_spec((d, f)), _const_spec((d, f)), _const_spec((f, d)),
                  _const_spec((1, d)), _const_spec((1, d))],
        out_specs=tok,
        out_shape=jax.ShapeDtypeStruct((n, d), F32),
        compiler_params=pltpu.CompilerParams(
            dimension_semantics=("arbitrary",), vmem_limit_bytes=VMEM_LIMIT_BYTES),
        name="dense_ffn",
    )(x2, w_gate, w_up, w_down, ln_g, ln_b)


SLAB = 8


def _slab_copy(src_ref, src_row, dst_ref, dst_row, sem):
    src = src_ref.at[pl.ds(pl.multiple_of(src_row * SLAB, SLAB), SLAB), :]
    dst = dst_ref.at[pl.ds(pl.multiple_of(dst_row * SLAB, SLAB), SLAB), :]
    return pltpu.make_async_copy(src, dst, sem)


def _moe_route_kernel(x_ref, router_ref, eidx_ref, slot_ref, wts_ref, counts_ref, carry_ref, *, tile):
    @pl.when(pl.program_id(0) == 0)
    def _():
        carry_ref[...] = jnp.zeros_like(carry_ref)

    xh, xm, xl = _split3(x_ref[...])
    rh, rm, rl = _split3(router_ref[...])
    logits = (_dot_nt(rh, xh) + _dot_nt(rm, xh) + _dot_nt(rh, xm)) + (
        _dot_nt(rl, xh) + _dot_nt(rm, xm) + _dot_nt(rh, xl))
    nrow = logits.shape[0]
    row = lax.broadcasted_iota(jnp.int32, logits.shape, 0)
    neg = jnp.finfo(F32).min
    logits = jnp.where(row < N_EXPERTS, logits, neg)
    m1 = jnp.max(logits, axis=0, keepdims=True)
    i1 = jnp.min(jnp.where(logits == m1, row, nrow), axis=0, keepdims=True)
    rest = jnp.where(row == i1, neg, logits)
    m2 = jnp.max(rest, axis=0, keepdims=True)
    i2 = jnp.min(jnp.where(rest == m2, row, nrow), axis=0, keepdims=True)
    e = jnp.exp(m2 - m1)
    w1 = 1.0 / (1.0 + e)
    w2 = e / (1.0 + e)

    chosen = (row == i1) | (row == i2)
    t_from = lax.broadcasted_iota(jnp.int32, (tile, tile), 0)
    t_to = lax.broadcasted_iota(jnp.int32, (tile, tile), 1)
    before = (t_from < t_to).astype(BF16)
    slot = carry_ref[...] + _dot(chosen.astype(BF16), before)
    carry_new = carry_ref[...] + jnp.sum(chosen.astype(F32), axis=1, keepdims=True)
    carry_ref[...] = carry_new
    eidx_ref[0, 0:1, :] = i1
    eidx_ref[0, 1:2, :] = i2
    slot_ref[0, 0:1, :] = jnp.sum(jnp.where(row == i1, slot, 0.0), axis=0, keepdims=True).astype(jnp.int32)
    slot_ref[0, 1:2, :] = jnp.sum(jnp.where(row == i2, slot, 0.0), axis=0, keepdims=True).astype(jnp.int32)
    counts_ref[...] = carry_new[:, 0:LANES].astype(jnp.int32)
    wrow = lax.broadcasted_iota(jnp.int32, (LANES, tile), 0)
    wts_ref[...] = jnp.where(wrow == 0, w1, jnp.where(wrow == 1, w2, 0.0)).T


def _moe_route(x2, router_t, tile):
    n, d = x2.shape
    rows = router_t.shape[0]
    idx_spec = pl.BlockSpec((1, 2, tile), lambda i: (i, 0, 0))
    idx_shape = jax.ShapeDtypeStruct((n // tile, 2, tile), jnp.int32)
    return pl.pallas_call(
        functools.partial(_moe_route_kernel, tile=tile),
        grid=(n // tile,),
        in_specs=[pl.BlockSpec((tile, d), lambda i: (i, 0)), _const_spec((rows, d))],
        out_specs=[idx_spec, idx_spec,
                   pl.BlockSpec((tile, LANES), lambda i: (i, 0)),
                   pl.BlockSpec((rows, LANES), lambda i: (0, 0))],
        out_shape=[idx_shape, idx_shape,
                   jax.ShapeDtypeStruct((n, LANES), F32),
                   jax.ShapeDtypeStruct((rows, LANES), jnp.int32)],
        scratch_shapes=[pltpu.VMEM((rows, tile), F32)],
        compiler_params=pltpu.CompilerParams(
            dimension_semantics=("arbitrary",), vmem_limit_bytes=VMEM_LIMIT_BYTES),
        name="moe_route",
    )(x2, router_t)


def _moe_dispatch_kernel(fill_lo_ref, fill_hi_ref, x_ref, dest_ref, xs_ref,
                         slab_ref, zero_ref, dest_smem, sem_idx, sem_rows, *, tile):
    idx_copy = pltpu.make_async_copy(dest_ref.at[0], dest_smem, sem_idx)
    idx_copy.start()
    for j in range(SLAB):
        slab_ref[pl.ds(j, tile, stride=SLAB), :] = x_ref[:, j * LANES:(j + 1) * LANES]
    idx_copy.wait()

    def send(t, carry):
        _slab_copy(slab_ref, t, xs_ref, dest_smem[0, t], sem_rows).start()
        _slab_copy(slab_ref, t, xs_ref, dest_smem[1, t], sem_rows).start()
        return carry

    def drain(t, carry):
        _slab_copy(slab_ref, 0, xs_ref, 0, sem_rows).wait()
        _slab_copy(slab_ref, 0, xs_ref, 0, sem_rows).wait()
        return carry

    lax.fori_loop(0, tile, send, 0, unroll=8)
    lax.fori_loop(0, tile, drain, 0, unroll=8)

    @pl.when(pl.program_id(0) == pl.num_programs(0) - 1)
    def _():
        zero_ref[...] = jnp.zeros_like(zero_ref)
        for k in range(N_EXPERTS + 1):
            lo = fill_lo_ref[k]
            hi = fill_hi_ref[k]

            def fill(r, carry):
                _slab_copy(zero_ref, 0, xs_ref, r, sem_rows).start()
                return carry

            def fill_wait(r, carry):
                _slab_copy(zero_ref, 0, xs_ref, 0, sem_rows).wait()
                return carry

            lax.fori_loop(lo, hi, fill, 0)
            lax.fori_loop(lo, hi, fill_wait, 0)


def _moe_dispatch(x2, dest, fill_lo, fill_hi, n_rows, tile):
    n, d = x2.shape
    grid_spec = pltpu.PrefetchScalarGridSpec(
        num_scalar_prefetch=2,
        grid=(n // tile,),
        in_specs=[pl.BlockSpec((tile, d), lambda i, lo, hi: (i, 0)),
                  pl.BlockSpec((1, 2, tile), lambda i, lo, hi: (i, 0, 0))],
        out_specs=pl.BlockSpec(memory_space=pl.ANY),
        scratch_shapes=[
            pltpu.VMEM((tile * SLAB, LANES), F32),
            pltpu.VMEM((SLAB, LANES), F32),
            pltpu.SMEM((2, tile), jnp.int32),
            pltpu.SemaphoreType.DMA,
            pltpu.SemaphoreType.DMA,
        ],
    )
    return pl.pallas_call(
        functools.partial(_moe_dispatch_kernel, tile=tile),
        grid_spec=grid_spec,
        out_shape=jax.ShapeDtypeStruct((n_rows * SLAB, LANES), F32),
        compiler_params=pltpu.CompilerParams(
            dimension_semantics=("arbitrary",), vmem_limit_bytes=VMEM_LIMIT_BYTES),
        name="moe_dispatch",
    )(fill_lo, fill_hi, x2, dest)


def _moe_expert_kernel(blk_ref, exp_ref, valid_ref, xs_ref, wg_ref, wu_ref, wd_ref, ys_ref, xb_ref, acc_ref,
                       *, row_tile):
    i = pl.program_id(0)
    f = pl.program_id(1)

    @pl.when(valid_ref[i] == 1)
    def _():
        @pl.when(f == 0)
        def _():
            for j in range(SLAB):
                xb_ref[:, j * LANES:(j + 1) * LANES] = xs_ref[pl.ds(j, row_tile, stride=SLAB), :].astype(BF16)
            acc_ref[...] = jnp.zeros_like(acc_ref)

        xb = xb_ref[...]
        h = _silu(_dot(xb, wg_ref[0])) * _dot(xb, wu_ref[0])
        acc_ref[...] += _dot(h.astype(BF16), wd_ref[0])

        @pl.when(f == pl.num_programs(1) - 1)
        def _():
            for j in range(SLAB):
                ys_ref[pl.ds(j, row_tile, stride=SLAB), :] = acc_ref[:, j * LANES:(j + 1) * LANES]

    @pl.when((valid_ref[i] == 0) & (f == 0))
    def _():
        ys_ref[...] = jnp.zeros_like(ys_ref)


def _moe_experts(xs, blk, exp, valid, w_gate, w_up, w_down, row_tile, ft):
    f = w_gate.shape[2]
    d = w_gate.shape[1]
    nf = f // ft
    n_steps = blk.shape[0]

    def f_idx(i, k, valid_ref):
        return jnp.where(valid_ref[i] == 1, k, nf - 1)

    rows_spec = pl.BlockSpec((row_tile * SLAB, LANES), lambda i, k, b, e, v: (b[i], 0))
    grid_spec = pltpu.PrefetchScalarGridSpec(
        num_scalar_prefetch=3,
        grid=(n_steps, nf),
        in_specs=[
            rows_spec,
            pl.BlockSpec((1, d, ft), lambda i, k, b, e, v: (e[i], 0, f_idx(i, k, v))),
            pl.BlockSpec((1, d, ft), lambda i, k, b, e, v: (e[i], 0, f_idx(i, k, v))),
            pl.BlockSpec((1, ft, d), lambda i, k, b, e, v: (e[i], f_idx(i, k, v), 0)),
        ],
        out_specs=rows_spec,
        scratch_shapes=[pltpu.VMEM((row_tile, d), BF16), pltpu.VMEM((row_tile, d), F32)],
    )
    return pl.pallas_call(
        functools.partial(_moe_expert_kernel, row_tile=row_tile),
        grid_spec=grid_spec,
        out_shape=jax.ShapeDtypeStruct(xs.shape, F32),
        compiler_params=pltpu.CompilerParams(
            dimension_semantics=("arbitrary", "arbitrary"), vmem_limit_bytes=VMEM_LIMIT_BYTES),
        name="moe_experts",
    )(blk, exp, valid, xs, w_gate, w_up, w_down)


def _moe_combine_kernel(x_ref, dest_ref, wts_ref, ys_ref, lng_ref, lnb_ref, out_ref,
                        y1_ref, y2_ref, dest_smem, sem_idx, sem_rows, *, tile):
    idx_copy = pltpu.make_async_copy(dest_ref.at[0], dest_smem, sem_idx)
    idx_copy.start()
    idx_copy.wait()

    def fetch(t, carry):
        _slab_copy(ys_ref, dest_smem[0, t], y1_ref, t, sem_rows).start()
        _slab_copy(ys_ref, dest_smem[1, t], y2_ref, t, sem_rows).start()
        return carry

    def drain(t, carry):
        _slab_copy(ys_ref, 0, y1_ref, 0, sem_rows).wait()
        _slab_copy(ys_ref, 0, y2_ref, 0, sem_rows).wait()
        return carry

    lax.fori_loop(0, tile, fetch, 0, unroll=8)
    lax.fori_loop(0, tile, drain, 0, unroll=8)

    w = wts_ref[...]
    w1 = w[:, 0:1]
    w2 = w[:, 1:2]
    f = jnp.concatenate(
        [w1 * y1_ref[pl.ds(j, tile, stride=SLAB), :] + w2 * y2_ref[pl.ds(j, tile, stride=SLAB), :]
         for j in range(SLAB)], axis=1)
    out_ref[...] = _layer_norm(DN_ALPHA * x_ref[...] + f, lng_ref[...], lnb_ref[...])


def _moe_combine(x2, dest, wts, ys, ln_g, ln_b, tile):
    n, d = x2.shape
    tok = pl.BlockSpec((tile, d), lambda i: (i, 0))
    return pl.pallas_call(
        functools.partial(_moe_combine_kernel, tile=tile),
        grid=(n // tile,),
        in_specs=[
            tok,
            pl.BlockSpec((1, 2, tile), lambda i: (i, 0, 0)),
            pl.BlockSpec((tile, LANES), lambda i: (i, 0)),
            pl.BlockSpec(memory_space=pl.ANY),
            _const_spec((1, d)), _const_spec((1, d)),
        ],
        out_specs=tok,
        out_shape=jax.ShapeDtypeStruct((n, d), F32),
        scratch_shapes=[
            pltpu.VMEM((tile * SLAB, LANES), F32),
            pltpu.VMEM((tile * SLAB, LANES), F32),
            pltpu.SMEM((2, tile), jnp.int32),
            pltpu.SemaphoreType.DMA,
            pltpu.SemaphoreType.DMA,
        ],
        compiler_params=pltpu.CompilerParams(
            dimension_semantics=("arbitrary",), vmem_limit_bytes=VMEM_LIMIT_BYTES),
        name="moe_combine",
    )(x2, dest, wts, ys, ln_g, ln_b)


def _moe_plan(counts, eidx, slot, n_tiles, row_tile):
    padded = ((counts + row_tile - 1) // row_tile) * row_tile
    ends = jnp.cumsum(padded)
    offs = ends - padded
    dest = offs[eidx] + slot
    used_tiles = ends[-1] // row_tile
    steps = jnp.arange(n_tiles, dtype=jnp.int32)
    valid = (steps < used_tiles).astype(jnp.int32)
    clipped = jnp.minimum(steps, used_tiles - 1)
    exp = jnp.sum((clipped[:, None] * row_tile >= ends[None, :]).astype(jnp.int32), axis=1)
    blk = steps
    fill_lo = jnp.concatenate([offs + counts, ends[-1:]])
    fill_hi = jnp.concatenate([ends, jnp.full((1,), n_tiles * row_tile, jnp.int32)])
    return (dest.astype(jnp.int32), blk, exp.astype(jnp.int32), valid,
            fill_lo.astype(jnp.int32), fill_hi.astype(jnp.int32))


def _moe_ffn(x2, router, w_gate, w_up, w_down, ln_g, ln_b, tile, row_tile, ft):
    n, d = x2.shape
    assert d == SLAB * LANES
    router_t = jnp.pad(router.T, ((0, 2 * SLAB - N_EXPERTS), (0, 0)))
    eidx, slot, wts, counts = _moe_route(x2, router_t, tile)
    n_tiles = (2 * n) // row_tile + N_EXPERTS
    dest, blk, exp, valid, fill_lo, fill_hi = _moe_plan(counts[:N_EXPERTS, 0], eidx, slot, n_tiles, row_tile)
    xs = _moe_dispatch(x2, dest, fill_lo, fill_hi, n_tiles * row_tile, tile)
    ys = _moe_experts(xs, blk, exp, valid, w_gate, w_up, w_down, row_tile, ft)
    return _moe_combine(x2, dest, wts, ys, ln_g, ln_b, tile)


def _layer_weights(layer, w_in, conv_w, conv_b, dt_bias, a_log, d_skip, ssd_norm_w, ret_gn_w, ret_gn_b,
                   w_ret_br, w_ssd_br, w_out, ln1_g, ln1_b):
    d = w_in.shape[1]
    parts, start = [], 0
    for width in IN_SPLITS:
        parts.append(w_in[layer, :, start:start + width])
        start += width
    wq, wk, wv, wgt, wz, wxbc, wdt, wgr, wgs = parts
    half = RET_DK // 2
    perm = jnp.concatenate([jnp.arange(half) * 2, jnp.arange(half) * 2 + 1])
    wq = wq.reshape(d, RET_HEADS, RET_DK)[:, :, perm]
    wk = wk.reshape(d, RET_HEADS, RET_DK)[:, :, perm]
    wv = wv.reshape(d, RET_HEADS, RET_DV)
    wgt = wgt.reshape(d, RET_HEADS, RET_DV)
    w_ret = jnp.transpose(jnp.concatenate([wq, wk, wv, wgt], axis=2), (1, 0, 2)).astype(BF16)
    pad_h = LANES - SSD_HEADS
    return {
        "w_ret": w_ret,
        "w_gate_ret": wgr.astype(BF16),
        "w_ret_br": w_ret_br[layer].reshape(RET_HEADS, RET_DV, d).astype(BF16),
        "ret_gn_w": ret_gn_w[layer].reshape(RET_HEADS, 1, RET_DV),
        "ret_gn_b": ret_gn_b[layer].reshape(RET_HEADS, 1, RET_DV),
        "w_z": wz.astype(BF16),
        "w_xbc": wxbc.astype(BF16),
        "w_dt": jnp.pad(wdt, ((0, 0), (0, pad_h))).astype(BF16),
        "w_gate_ssd": wgs.astype(BF16),
        "w_ssd_br": w_ssd_br[layer].astype(BF16),
        "w_out": w_out[layer].astype(BF16),
        "conv_w": conv_w[layer],
        "conv_b": conv_b[layer].reshape(1, SSD_CONV_DIM),
        "dt_bias": jnp.pad(dt_bias[layer], (0, pad_h)).reshape(1, LANES),
        "a_log": jnp.pad(a_log[layer], (0, pad_h)).reshape(1, LANES),
        "d_skip": jnp.repeat(d_skip[layer], SSD_HEAD_DIM).reshape(1, SSD_INNER),
        "ssd_norm_w": ssd_norm_w[layer].reshape(1, SSD_INNER),
        "ln1_g": ln1_g[layer].reshape(1, d),
        "ln1_b": ln1_b[layer].reshape(1, d),
    }


def _forward(x, positions, w_in, conv_w, conv_b, dt_bias, a_log, d_skip, ssd_norm_w, ret_gn_w, ret_gn_b,
             w_ret_br, w_ssd_br, w_out, ln1_g, ln1_b, ln2_g, ln2_b, dense_w_gate, dense_w_up, dense_w_down,
             moe_router, moe_w_gate, moe_w_up, moe_w_down, *, ret_tile, ret_chunk, ssd_tile, ssd_chunk,
             ffn_tile, moe_row_tile, moe_ft):
    b, s, d = x.shape
    pos3 = positions.reshape(b, s, 1)
    for layer in range(w_in.shape[0]):
        lw = _layer_weights(layer, w_in, conv_w, conv_b, dt_bias, a_log, d_skip, ssd_norm_w, ret_gn_w,
                            ret_gn_b, w_ret_br, w_ssd_br, w_out, ln1_g, ln1_b)
        yret = _retention_branch(x, pos3, lw, ret_tile, ret_chunk)
        x = _ssd_branch(x, yret, lw, ssd_tile, ssd_chunk)
        x2 = x.reshape(b * s, d)
        g2 = ln2_g[layer].reshape(1, d)
        b2 = ln2_b[layer].reshape(1, d)
        i = layer // 2
        if layer % 2 == 0:
            x2 = _dense_ffn(x2, dense_w_gate[i].astype(BF16), dense_w_up[i].astype(BF16),
                            dense_w_down[i].astype(BF16), g2, b2, ffn_tile)
        else:
            x2 = _moe_ffn(x2, moe_router[i], moe_w_gate[i].astype(BF16), moe_w_up[i].astype(BF16),
                          moe_w_down[i].astype(BF16), g2, b2, ffn_tile, moe_row_tile, moe_ft)
        x = x2.reshape(b, s, d)
    return x


def kernel(x, positions, w_in, conv_w, conv_b, dt_bias, a_log, d_skip, ssd_norm_w, ret_gn_w, ret_gn_b, w_ret_br, w_ssd_br, w_out, ln1_g, ln1_b, ln2_g, ln2_b, dense_w_gate, dense_w_up, dense_w_down, moe_router, moe_w_gate, moe_w_up, moe_w_down):
    return _forward(x, positions, w_in, conv_w, conv_b, dt_bias, a_log, d_skip, ssd_norm_w, ret_gn_w, ret_gn_b,
                    w_ret_br, w_ssd_br, w_out, ln1_g, ln1_b, ln2_g, ln2_b, dense_w_gate, dense_w_up,
                    dense_w_down, moe_router, moe_w_gate, moe_w_up, moe_w_down,
                    ret_tile=RET_TILE, ret_chunk=RET_CHUNK, ssd_tile=SSD_TILE, ssd_chunk=SSD_CHUNK,
                    ffn_tile=FFN_TILE, moe_row_tile=MOE_ROW_TILE, moe_ft=MOE_FT)
```

```python
import functools
import math

import jax
import jax.numpy as jnp
from jax import lax
from jax.experimental import pallas as pl
from jax.experimental.pallas import tpu as pltpu

F32 = jnp.float32
BF16 = jnp.bfloat16

D_MODEL = 1024
DEPTH = 4

RET_HEADS = 4
RET_DK = 256
RET_DV = 512
RET_QK = RET_HEADS * RET_DK
RET_V = RET_HEADS * RET_DV
RET_HEAD_COLS = 2 * RET_DK + 2 * RET_DV
ROPE_THETA = 10000.0

SSD_INNER = 2048
SSD_HEAD_DIM = 64
SSD_HEADS = 32
SSD_GROUPS = 8
SSD_STATE = 128
SSD_CONV = 4
SSD_CONV_DIM = SSD_INNER + 2 * SSD_GROUPS * SSD_STATE
SSD_PAIRS = SSD_HEADS // 2
SSD_GROUP_COLS = SSD_INNER // SSD_GROUPS
SSD_UNIT_COLS = 256

IN_SPLITS = (RET_QK, RET_QK, RET_V, RET_V, SSD_INNER, SSD_CONV_DIM, SSD_HEADS, D_MODEL, D_MODEL)

N_EXPERTS = 8
DN_ALPHA = (2 * DEPTH) ** 0.25
LN_EPS = 1e-5
GN_EPS = 1e-5
RMS_EPS = 1e-5

LANES = 128
CONV_HALO = 8
VMEM_LIMIT_BYTES = 56 * 1024 * 1024

RET_TILE = 512
RET_CHUNK = 256
SSD_TILE = 256
SSD_CHUNK = 128
FFN_TILE = 512
MOE_ROW_TILE = 1024
MOE_FT = 512


def _dot(a, b):
    return jnp.dot(a, b, preferred_element_type=F32)


def _dot_nt(a, b):
    return lax.dot_general(a, b, (((1,), (1,)), ((), ())), preferred_element_type=F32)


def _dot_tn(a, b):
    return lax.dot_general(a, b, (((0,), (0,)), ((), ())), preferred_element_type=F32)


def _sigmoid(v):
    return 1.0 / (1.0 + jnp.exp(-v))


def _silu(v):
    return v * _sigmoid(v)


def _layer_norm(v, g, b):
    mu = jnp.mean(v, axis=-1, keepdims=True)
    d = v - mu
    var = jnp.mean(d * d, axis=-1, keepdims=True)
    return d * lax.rsqrt(var + LN_EPS) * g + b


def _split3(v):
    hi = v.astype(BF16)
    r1 = v - hi.astype(F32)
    mid = r1.astype(BF16)
    lo = (r1 - mid.astype(F32)).astype(BF16)
    return hi, mid, lo


def _const_spec(shape):
    nd = len(shape)
    return pl.BlockSpec(shape, lambda *_: (0,) * nd, pipeline_mode=pl.Buffered(1))


def _ret_kernel(x_ref, pos_ref, invf_ref, w_ref, wg_ref, wbr_ref, gnw_ref, gnb_ref,
                intra_ref, qdec_ref, kdec_ref, cdec_ref, out_ref,
                state_ref, xb_ref, cs_ref, proj0_ref, proj1_ref, ret0_ref, ret1_ref, *, tile, chunk):
    half = RET_DK // 2

    @pl.when(pl.program_id(1) == 0)
    def _():
        state_ref[...] = jnp.zeros_like(state_ref)

    xb_ref[...] = x_ref[0].astype(BF16)
    ang = pos_ref[0].astype(F32) * invf_ref[...]
    cs_ref[0] = jnp.cos(ang)
    cs_ref[1] = jnp.sin(ang)
    acc = None

    for h in range(RET_HEADS):
        proj_ref = (proj0_ref, proj1_ref)[h % 2]
        ret_ref = (ret0_ref, ret1_ref)[h % 2]
        proj_ref[...] = _dot(xb_ref[...], w_ref[h])
        for ci in range(tile // chunk):
            rows = slice(ci * chunk, (ci + 1) * chunk)
            cos = cs_ref[0, rows, :]
            sin = cs_ref[1, rows, :]
            q1 = proj_ref[rows, 0:half]
            q2 = proj_ref[rows, half:RET_DK]
            k1 = proj_ref[rows, RET_DK:RET_DK + half]
            k2 = proj_ref[rows, RET_DK + half:2 * RET_DK]
            v = proj_ref[rows, 2 * RET_DK:2 * RET_DK + RET_DV].astype(BF16)
            g = proj_ref[rows, 2 * RET_DK + RET_DV:RET_HEAD_COLS]
            q = jnp.concatenate([q1 * cos - q2 * sin, q1 * sin + q2 * cos], axis=1)
            k = jnp.concatenate([k1 * cos - k2 * sin, k1 * sin + k2 * cos], axis=1) * (RET_DK ** -0.5)
            scores = _dot_nt(q.astype(BF16), k.astype(BF16)) * intra_ref[h]
            inner = _dot(scores.astype(BF16), v)
            st = state_ref[h]
            cross = _dot((q * qdec_ref[h]).astype(BF16), st.astype(BF16))
            state_ref[h] = st * cdec_ref[h] + _dot_tn((k * kdec_ref[h]).astype(BF16), v)
            o = inner + cross
            mu = jnp.mean(o, axis=-1, keepdims=True)
            d = o - mu
            var = jnp.mean(d * d, axis=-1, keepdims=True)
            o = d * lax.rsqrt(var + GN_EPS) * gnw_ref[h] + gnb_ref[h]
            ret_ref[rows, :] = (_silu(g) * o).astype(BF16)
        part = _dot(ret_ref[...], wbr_ref[h])
        acc = part if acc is None else acc + part

    gate = _sigmoid(_dot(xb_ref[...], wg_ref[...]))
    out_ref[0] = gate * acc


def _ret_tables(chunk):
    log_gamma = jnp.log(1.0 - 2.0 ** (-5.0 - jnp.arange(RET_HEADS, dtype=F32)))
    pos = jnp.arange(chunk, dtype=F32)
    diff = pos[:, None] - pos[None, :]
    causal = diff >= 0
    intra = jnp.where(causal[None], jnp.exp(log_gamma[:, None, None] * jnp.where(causal, diff, 0.0)[None]), 0.0)
    qdec = jnp.exp(log_gamma[:, None] * (pos[None, :] + 1.0))
    kdec = jnp.exp(log_gamma[:, None] * (chunk - 1.0 - pos[None, :]))
    cdec = jnp.exp(log_gamma * chunk)
    qdec = jnp.broadcast_to(qdec[:, :, None], (RET_HEADS, chunk, RET_DK))
    kdec = jnp.broadcast_to(kdec[:, :, None], (RET_HEADS, chunk, RET_DK))
    cdec = jnp.broadcast_to(cdec[:, None, None], (RET_HEADS, 1, RET_DV))
    return intra, qdec, kdec, cdec


def _retention_branch(x, pos3, lw, tile, chunk):
    b, s, d = x.shape
    half = RET_DK // 2
    inv_freq = (ROPE_THETA ** (-jnp.arange(half, dtype=F32) / half)).reshape(1, half)
    intra, qdec, kdec, cdec = _ret_tables(chunk)
    kern = functools.partial(_ret_kernel, tile=tile, chunk=chunk)
    return pl.pallas_call(
        kern,
        grid=(b, s // tile),
        in_specs=[
            pl.BlockSpec((1, tile, d), lambda i, j: (i, j, 0)),
            pl.BlockSpec((1, tile, 1), lambda i, j: (i, j, 0)),
            _const_spec((1, half)),
            _const_spec((RET_HEADS, d, RET_HEAD_COLS)),
            _const_spec((d, d)),
            _const_spec((RET_HEADS, RET_DV, d)),
            _const_spec((RET_HEADS, 1, RET_DV)),
            _const_spec((RET_HEADS, 1, RET_DV)),
            _const_spec((RET_HEADS, chunk, chunk)),
            _const_spec((RET_HEADS, chunk, RET_DK)),
            _const_spec((RET_HEADS, chunk, RET_DK)),
            _const_spec((RET_HEADS, 1, RET_DV)),
        ],
        out_specs=pl.BlockSpec((1, tile, d), lambda i, j: (i, j, 0)),
        out_shape=jax.ShapeDtypeStruct((b, s, d), F32),
        scratch_shapes=[
            pltpu.VMEM((RET_HEADS, RET_DK, RET_DV), F32),
            pltpu.VMEM((tile, d), BF16),
            pltpu.VMEM((2, tile, half), F32),
            pltpu.VMEM((tile, RET_HEAD_COLS), F32),
            pltpu.VMEM((tile, RET_HEAD_COLS), F32),
            pltpu.VMEM((tile, RET_DV), BF16),
            pltpu.VMEM((tile, RET_DV), BF16),
        ],
        compiler_params=pltpu.CompilerParams(
            dimension_semantics=("arbitrary", "arbitrary"), vmem_limit_bytes=VMEM_LIMIT_BYTES),
        name="retention_branch",
    )(x, pos3, inv_freq, lw["w_ret"], lw["w_gate_ret"], lw["w_ret_br"], lw["ret_gn_w"], lw["ret_gn_b"],
      intra, qdec, kdec, cdec)


def _col_blocks(width):
    return [slice(c, c + SSD_UNIT_COLS) for c in range(0, width, SSD_UNIT_COLS)]


def _ssd_project_units(xb_ref, wz_ref, wxbc_ref, wdt_ref, wg_ref, convw_ref, convb_ref, dtb_ref,
                       raw_ref, xbc_ref, z_ref, dt_ref, gate_ref, *, tile):
    def xbc_unit(cols):
        def run():
            raw_ref[CONV_HALO:CONV_HALO + tile, cols] = _dot(xb_ref[...], wxbc_ref[:, cols])
        return run

    def conv_unit(cols):
        def run():
            full = raw_ref[:, cols]
            conv = convw_ref[0:1, cols] * full
            for k in range(1, SSD_CONV):
                conv = pltpu.roll(conv, 1, 0) + convw_ref[k:k + 1, cols] * full
            xbc_ref[:, cols] = _silu(conv[CONV_HALO:, :] + convb_ref[:, cols])
            raw_ref[0:CONV_HALO, cols] = raw_ref[tile:tile + CONV_HALO, cols]
        return run

    def z_unit(cols):
        def run():
            z_ref[:, cols] = _dot(xb_ref[...], wz_ref[:, cols])
        return run

    def gate_unit(cols):
        def run():
            gate_ref[:, cols] = _sigmoid(_dot(xb_ref[...], wg_ref[:, cols]))
        return run

    def dt_unit():
        dtr = _dot(xb_ref[...], wdt_ref[...]) + dtb_ref[...]
        dt_ref[...] = jnp.maximum(dtr, 0.0) + jnp.log1p(jnp.exp(-jnp.abs(dtr)))

    matmul_units = ([xbc_unit(c) for c in _col_blocks(SSD_CONV_DIM)] + [z_unit(c) for c in _col_blocks(SSD_INNER)]
                    + [gate_unit(c) for c in _col_blocks(D_MODEL)] + [dt_unit])
    conv_units = [conv_unit(c) for c in _col_blocks(SSD_CONV_DIM)]
    return matmul_units, conv_units


def _ssd_scan(x_ref, yret_ref, xbc_ref, z_ref, dt_ref, gate_ref, wbr_ref, wout_ref, alog_ref, dskip_ref,
              normw_ref, lng_ref, lnb_ref, out_ref, y_ref, state_ref, side_units, tail_units, *, tile, chunk):
    side_units = list(side_units)
    tail_units = list(tail_units)
    a_neg = -jnp.exp(alog_ref[...])
    rows_i = lax.broadcasted_iota(jnp.int32, (chunk, chunk), 0)
    cols_i = lax.broadcasted_iota(jnp.int32, (chunk, chunk), 1)
    tri = rows_i >= cols_i
    tril_bf = tri.astype(BF16)
    low_half = lax.broadcasted_iota(jnp.int32, (1, LANES), 1) < SSD_HEAD_DIM
    b_off = SSD_INNER
    c_off = SSD_INNER + SSD_GROUPS * SSD_STATE

    for ci in range(tile // chunk):
        rows = slice(ci * chunk, (ci + 1) * chunk)
        dt_c = dt_ref[rows, :]
        hi, mid, lo = _split3(dt_c * a_neg)
        a_cs = _dot(tril_bf, hi) + _dot(tril_bf, mid) + _dot(tril_bf, lo)
        a_row = a_cs.T
        dt_row = dt_c.T
        wdt_row = jnp.exp(a_row[:, chunk - 1:chunk] - a_row) * dt_row
        for g in range(SSD_GROUPS):
            bc = xbc_ref[rows, b_off + g * SSD_STATE:b_off + (g + 1) * SSD_STATE]
            cc = xbc_ref[rows, c_off + g * SSD_STATE:c_off + (g + 1) * SSD_STATE]
            cbm = jnp.where(tri, _dot_nt(cc.astype(BF16), bc.astype(BF16)), 0.0)
            bct = bc.T
            ys = []
            for p in range(2):
                pair = 2 * g + p
                hd0 = 2 * pair
                hd1 = hd0 + 1
                xs_p = xbc_ref[rows, pair * LANES:(pair + 1) * LANES]
                col0 = jnp.broadcast_to(a_cs[:, hd0:hd0 + 1], (chunk, LANES))
                col1 = jnp.broadcast_to(a_cs[:, hd1:hd1 + 1], (chunk, LANES))
                l0 = cbm * jnp.exp(jnp.minimum(col0 - a_row[hd0:hd0 + 1, :], 0.0)) * dt_row[hd0:hd0 + 1, :]
                l1 = cbm * jnp.exp(jnp.minimum(col1 - a_row[hd1:hd1 + 1, :], 0.0)) * dt_row[hd1:hd1 + 1, :]
                e0 = jnp.exp(col0)
                e1 = jnp.exp(col1)
                lhs_y = jnp.concatenate([l0, l1, cc * e0, cc * e1], axis=1).astype(BF16)
                st = state_ref[pair]
                xs_lo = jnp.where(low_half, xs_p, 0.0)
                xs_hi = jnp.where(low_half, 0.0, xs_p)
                st_lo = jnp.where(low_half, st, 0.0)
                st_hi = jnp.where(low_half, 0.0, st)
                rhs = jnp.concatenate([xs_lo, xs_hi, st_lo, st_hi], axis=0).astype(BF16)
                y = _dot(lhs_y, rhs) + dskip_ref[:, pair * LANES:(pair + 1) * LANES] * xs_p
                lhs_s = jnp.concatenate(
                    [bct * wdt_row[hd0:hd0 + 1, :], bct * wdt_row[hd1:hd1 + 1, :]], axis=1).astype(BF16)
                e_last = jnp.where(low_half, e0[chunk - 1:chunk, :], e1[chunk - 1:chunk, :])
                state_ref[pair] = st * e_last + _dot(lhs_s, rhs[0:2 * chunk, :])
                ys.append(y)
                if side_units:
                    side_units.pop(0)()
            gcols = slice(g * SSD_GROUP_COLS, (g + 1) * SSD_GROUP_COLS)
            yg = jnp.concatenate(ys, axis=1) * _silu(z_ref[rows, gcols])
            ms = jnp.mean(yg * yg, axis=-1, keepdims=True)
            y_ref[rows, gcols] = (yg * lax.rsqrt(ms + RMS_EPS) * normw_ref[:, gcols]).astype(BF16)

    for unit in side_units:
        unit()
    out_blocks = _col_blocks(D_MODEL)
    per_block = -(-len(tail_units) // (2 * len(out_blocks)))
    merged = []
    for c in out_blocks:
        merged.append((gate_ref[:, c] * _dot(y_ref[...], wbr_ref[:, c]) + yret_ref[0, :, c]).astype(BF16))
        for unit in tail_units[:per_block]:
            unit()
        del tail_units[:per_block]
    merged = jnp.concatenate(merged, axis=1)
    mixed = []
    for c in out_blocks:
        mixed.append(_dot(merged, wout_ref[:, c]))
        for unit in tail_units[:per_block]:
            unit()
        del tail_units[:per_block]
    for unit in tail_units:
        unit()
    out_ref[0] = _layer_norm(DN_ALPHA * x_ref[0] + jnp.concatenate(mixed, axis=1), lng_ref[...], lnb_ref[...])


def _ssd_kernel(xnext_ref, x_ref, yret_ref, wz_ref, wxbc_ref, wdt_ref, wg_ref, wbr_ref, wout_ref,
                convw_ref, convb_ref, dtb_ref, alog_ref, dskip_ref, normw_ref, lng_ref, lnb_ref,
                out_ref, raw_ref, xbc0_ref, xbc1_ref, z0_ref, z1_ref, dt0_ref, dt1_ref, gate0_ref, gate1_ref,
                y_ref, state_ref, xb_ref, *, tile, chunk):
    i = pl.program_id(0)
    j = pl.program_id(1)
    staged = ((xbc0_ref, z0_ref, dt0_ref, gate0_ref), (xbc1_ref, z1_ref, dt1_ref, gate1_ref))

    @pl.when(j == 0)
    def _():
        raw_ref[0:CONV_HALO, :] = jnp.zeros((CONV_HALO, SSD_CONV_DIM), F32)

    @pl.when((i == 0) & (j == 0))
    def _():
        for ref in staged[1]:
            ref[...] = jnp.zeros_like(ref)

    @pl.when(j <= 1)
    def _():
        state_ref[...] = jnp.zeros_like(state_ref)

    xb_ref[...] = xnext_ref[0].astype(BF16)

    for parity in range(2):
        @pl.when(lax.rem(j, 2) == parity)
        def _(parity=parity):
            matmul_units, conv_units = _ssd_project_units(
                xb_ref, wz_ref, wxbc_ref, wdt_ref, wg_ref, convw_ref, convb_ref, dtb_ref,
                raw_ref, *staged[parity], tile=tile)
            _ssd_scan(x_ref, yret_ref, *staged[1 - parity], wbr_ref, wout_ref, alog_ref, dskip_ref, normw_ref,
                      lng_ref, lnb_ref, out_ref, y_ref, state_ref, matmul_units, conv_units, tile=tile, chunk=chunk)


def _ssd_branch(x, yret, lw, tile, chunk):
    b, s, d = x.shape
    n_tiles = s // tile
    kern = functools.partial(_ssd_kernel, tile=tile, chunk=chunk)
    nxt = pl.BlockSpec((1, tile, d), lambda i, j: (i, jnp.minimum(j, n_tiles - 1), 0))
    cur = pl.BlockSpec((1, tile, d), lambda i, j: (i, jnp.maximum(j - 1, 0), 0))
    staging = [pltpu.VMEM((tile, SSD_CONV_DIM), F32)] * 2
    staging += [pltpu.VMEM((tile, SSD_INNER), F32)] * 2
    staging += [pltpu.VMEM((tile, LANES), F32)] * 2
    staging += [pltpu.VMEM((tile, d), F32)] * 2
    return pl.pallas_call(
        kern,
        grid=(b, n_tiles + 1),
        in_specs=[
            nxt, cur, cur,
            _const_spec((d, SSD_INNER)),
            _const_spec((d, SSD_CONV_DIM)),
            _const_spec((d, LANES)),
            _const_spec((d, d)),
            _const_spec((SSD_INNER, d)),
            _const_spec((d, d)),
            _const_spec((SSD_CONV, SSD_CONV_DIM)),
            _const_spec((1, SSD_CONV_DIM)),
            _const_spec((1, LANES)),
            _const_spec((1, LANES)),
            _const_spec((1, SSD_INNER)),
            _const_spec((1, SSD_INNER)),
            _const_spec((1, d)),
            _const_spec((1, d)),
        ],
        out_specs=cur,
        out_shape=jax.ShapeDtypeStruct((b, s, d), F32),
        scratch_shapes=[
            pltpu.VMEM((tile + CONV_HALO, SSD_CONV_DIM), F32),
            *staging,
            pltpu.VMEM((tile, SSD_INNER), BF16),
            pltpu.VMEM((SSD_PAIRS, SSD_STATE, LANES), F32),
            pltpu.VMEM((tile, d), BF16),
        ],
        compiler_params=pltpu.CompilerParams(
            dimension_semantics=("arbitrary", "arbitrary"), vmem_limit_bytes=VMEM_LIMIT_BYTES),
        name="ssd_branch",
    )(x, x, yret, lw["w_z"], lw["w_xbc"], lw["w_dt"], lw["w_gate_ssd"], lw["w_ssd_br"], lw["w_out"],
      lw["conv_w"], lw["conv_b"], lw["dt_bias"], lw["a_log"], lw["d_skip"], lw["ssd_norm_w"],
      lw["ln1_g"], lw["ln1_b"])


def _dense_ffn_kernel(x_ref, wg_ref, wu_ref, wd_ref, lng_ref, lnb_ref, out_ref):
    x = x_ref[...]
    xb = x.astype(BF16)
    h = _silu(_dot(xb, wg_ref[...])) * _dot(xb, wu_ref[...])
    f = _dot(h.astype(BF16), wd_ref[...])
    out_ref[...] = _layer_norm(DN_ALPHA * x + f, lng_ref[...], lnb_ref[...])


def _dense_ffn(x2, w_gate, w_up, w_down, ln_g, ln_b, tile):
    n, d = x2.shape
    f = w_gate.shape[1]
    tok = pl.BlockSpec((tile, d), lambda i: (i, 0))
    return pl.pallas_call(
        _dense_ffn_kernel,
        grid=(n // tile,),
        in_specs=[tok, _const_spec((d, f)), _const_spec((d, f)), _const_spec((f, d)),
                  _const_spec((1, d)), _const_spec((1, d))],
        out_specs=tok,
        out_shape=jax.ShapeDtypeStruct((n, d), F32),
        compiler_params=pltpu.CompilerParams(
            dimension_semantics=("arbitrary",), vmem_limit_bytes=VMEM_LIMIT_BYTES),
        name="dense_ffn",
    )(x2, w_gate, w_up, w_down, ln_g, ln_b)


SLAB = 8


def _slab_copy(src_ref, src_row, dst_ref, dst_row, sem):
    src = src_ref.at[pl.ds(pl.multiple_of(src_row * SLAB, SLAB), SLAB), :]
    dst = dst_ref.at[pl.ds(pl.multiple_of(dst_row * SLAB, SLAB), SLAB), :]
    return pltpu.make_async_copy(src, dst, sem)


def _moe_route_kernel(x_ref, router_ref, eidx_ref, slot_ref, wts_ref, counts_ref, carry_ref, *, tile):
    @pl.when(pl.program_id(0) == 0)
    def _():
        carry_ref[...] = jnp.zeros_like(carry_ref)

    xh, xm, xl = _split3(x_ref[...])
    rh, rm, rl = _split3(router_ref[...])
    logits = (_dot_nt(rh, xh) + _dot_nt(rm, xh) + _dot_nt(rh, xm)) + (
        _dot_nt(rl, xh) + _dot_nt(rm, xm) + _dot_nt(rh, xl))
    nrow = logits.shape[0]
    row = lax.broadcasted_iota(jnp.int32, logits.shape, 0)
    neg = jnp.finfo(F32).min
    logits = jnp.where(row < N_EXPERTS, logits, neg)
    m1 = jnp.max(logits, axis=0, keepdims=True)
    i1 = jnp.min(jnp.where(logits == m1, row, nrow), axis=0, keepdims=True)
    rest = jnp.where(row == i1, neg, logits)
    m2 = jnp.max(rest, axis=0, keepdims=True)
    i2 = jnp.min(jnp.where(rest == m2, row, nrow), axis=0, keepdims=True)
    e = jnp.exp(m2 - m1)
    w1 = 1.0 / (1.0 + e)
    w2 = e / (1.0 + e)

    chosen = (row == i1) | (row == i2)
    t_from = lax.broadcasted_iota(jnp.int32, (tile, tile), 0)
    t_to = lax.broadcasted_iota(jnp.int32, (tile, tile), 1)
    before = (t_from < t_to).astype(BF16)
    slot = carry_ref[...] + _dot(chosen.astype(BF16), before)
    carry_new = carry_ref[...] + jnp.sum(chosen.astype(F32), axis=1, keepdims=True)
    carry_ref[...] = carry_new
    eidx_ref[0, 0:1, :] = i1
    eidx_ref[0, 1:2, :] = i2
    slot_ref[0, 0:1, :] = jnp.sum(jnp.where(row == i1, slot, 0.0), axis=0, keepdims=True).astype(jnp.int32)
    slot_ref[0, 1:2, :] = jnp.sum(jnp.where(row == i2, slot, 0.0), axis=0, keepdims=True).astype(jnp.int32)
    counts_ref[...] = carry_new[:, 0:LANES].astype(jnp.int32)
    wrow = lax.broadcasted_iota(jnp.int32, (LANES, tile), 0)
    wts_ref[...] = jnp.where(wrow == 0, w1, jnp.where(wrow == 1, w2, 0.0)).T


def _moe_route(x2, router_t, tile):
    n, d = x2.shape
    rows = router_t.shape[0]
    idx_spec = pl.BlockSpec((1, 2, tile), lambda i: (i, 0, 0))
    idx_shape = jax.ShapeDtypeStruct((n // tile, 2, tile), jnp.int32)
    return pl.pallas_call(
        functools.partial(_moe_route_kernel, tile=tile),
        grid=(n // tile,),
        in_specs=[pl.BlockSpec((tile, d), lambda i: (i, 0)), _const_spec((rows, d))],
        out_specs=[idx_spec, idx_spec,
                   pl.BlockSpec((tile, LANES), lambda i: (i, 0)),
                   pl.BlockSpec((rows, LANES), lambda i: (0, 0))],
        out_shape=[idx_shape, idx_shape,
                   jax.ShapeDtypeStruct((n, LANES), F32),
                   jax.ShapeDtypeStruct((rows, LANES), jnp.int32)],
        scratch_shapes=[pltpu.VMEM((rows, tile), F32)],
        compiler_params=pltpu.CompilerParams(
            dimension_semantics=("arbitrary",), vmem_limit_bytes=VMEM_LIMIT_BYTES),
        name="moe_route",
    )(x2, router_t)


def _moe_dispatch_kernel(fill_lo_ref, fill_hi_ref, x_ref, dest_ref, xs_ref,
                         slab0_ref, slab1_ref, zero_ref, dest_smem, sem_idx, sem0, sem1, sem_fill, *, tile):
    i = pl.program_id(0)
    last = pl.num_programs(0) - 1
    slots = ((slab0_ref, sem0), (slab1_ref, sem1))

    def drain(slab_ref, sem):
        def body(t, carry):
            _slab_copy(slab_ref, 0, xs_ref, 0, sem).wait()
            _slab_copy(slab_ref, 0, xs_ref, 0, sem).wait()
            return carry
        lax.fori_loop(0, tile, body, 0, unroll=8)

    for parity in range(2):
        @pl.when(lax.rem(i, 2) == parity)
        def _(parity=parity):
            slab_ref, sem = slots[parity]
            idx_copy = pltpu.make_async_copy(dest_ref.at[0], dest_smem, sem_idx)
            idx_copy.start()
            for j in range(SLAB):
                slab_ref[pl.ds(j, tile, stride=SLAB), :] = x_ref[:, j * LANES:(j + 1) * LANES]
            idx_copy.wait()

            def send(t, carry):
                _slab_copy(slab_ref, t, xs_ref, dest_smem[0, t], sem).start()
                _slab_copy(slab_ref, t, xs_ref, dest_smem[1, t], sem).start()
                return carry

            lax.fori_loop(0, tile, send, 0, unroll=8)

            @pl.when(i > 0)
            def _():
                drain(*slots[1 - parity])

            @pl.when(i == last)
            def _():
                drain(slab_ref, sem)

    @pl.when(i == last)
    def _():
        zero_ref[...] = jnp.zeros_like(zero_ref)
        for k in range(N_EXPERTS + 1):
            lo = fill_lo_ref[k]
            hi = fill_hi_ref[k]

            def fill(r, carry):
                _slab_copy(zero_ref, 0, xs_ref, r, sem_fill).start()
                return carry

            def fill_wait(r, carry):
                _slab_copy(zero_ref, 0, xs_ref, 0, sem_fill).wait()
                return carry

            lax.fori_loop(lo, hi, fill, 0)
            lax.fori_loop(lo, hi, fill_wait, 0)


def _moe_dispatch(x2, dest, fill_lo, fill_hi, n_rows, tile):
    n, d = x2.shape
    grid_spec = pltpu.PrefetchScalarGridSpec(
        num_scalar_prefetch=2,
        grid=(n // tile,),
        in_specs=[pl.BlockSpec((tile, d), lambda i, lo, hi: (i, 0)),
                  pl.BlockSpec((1, 2, tile), lambda i, lo, hi: (i, 0, 0))],
        out_specs=pl.BlockSpec(memory_space=pl.ANY),
        scratch_shapes=[
            pltpu.VMEM((tile * SLAB, LANES), F32),
            pltpu.VMEM((tile * SLAB, LANES), F32),
            pltpu.VMEM((SLAB, LANES), F32),
            pltpu.SMEM((2, tile), jnp.int32),
            pltpu.SemaphoreType.DMA,
            pltpu.SemaphoreType.DMA,
            pltpu.SemaphoreType.DMA,
            pltpu.SemaphoreType.DMA,
        ],
    )
    return pl.pallas_call(
        functools.partial(_moe_dispatch_kernel, tile=tile),
        grid_spec=grid_spec,
        out_shape=jax.ShapeDtypeStruct((n_rows * SLAB, LANES), F32),
        compiler_params=pltpu.CompilerParams(
            dimension_semantics=("arbitrary",), vmem_limit_bytes=VMEM_LIMIT_BYTES),
        name="moe_dispatch",
    )(fill_lo, fill_hi, x2, dest)


def _moe_expert_kernel(blk_ref, exp_ref, valid_ref, xs_ref, wg_ref, wu_ref, wd_ref, ys_ref, xb_ref, acc_ref,
                       *, row_tile):
    i = pl.program_id(0)
    f = pl.program_id(1)

    @pl.when(valid_ref[i] == 1)
    def _():
        @pl.when(f == 0)
        def _():
            for j in range(SLAB):
                xb_ref[:, j * LANES:(j + 1) * LANES] = xs_ref[pl.ds(j, row_tile, stride=SLAB), :].astype(BF16)
            acc_ref[...] = jnp.zeros_like(acc_ref)

        xb = xb_ref[...]
        h = _silu(_dot(xb, wg_ref[0])) * _dot(xb, wu_ref[0])
        acc_ref[...] += _dot(h.astype(BF16), wd_ref[0])

        @pl.when(f == pl.num_programs(1) - 1)
        def _():
            for j in range(SLAB):
                ys_ref[pl.ds(j, row_tile, stride=SLAB), :] = acc_ref[:, j * LANES:(j + 1) * LANES]

    @pl.when((valid_ref[i] == 0) & (f == 0))
    def _():
        ys_ref[...] = jnp.zeros_like(ys_ref)


def _moe_experts(xs, blk, exp, valid, w_gate, w_up, w_down, row_tile, ft):
    f = w_gate.shape[2]
    d = w_gate.shape[1]
    nf = f // ft
    n_steps = blk.shape[0]

    def f_idx(i, k, valid_ref):
        return jnp.where(valid_ref[i] == 1, k, nf - 1)

    rows_spec = pl.BlockSpec((row_tile * SLAB, LANES), lambda i, k, b, e, v: (b[i], 0))
    grid_spec = pltpu.PrefetchScalarGridSpec(
        num_scalar_prefetch=3,
        grid=(n_steps, nf),
        in_specs=[
            rows_spec,
            pl.BlockSpec((1, d, ft), lambda i, k, b, e, v: (e[i], 0, f_idx(i, k, v))),
            pl.BlockSpec((1, d, ft), lambda i, k, b, e, v: (e[i], 0, f_idx(i, k, v))),
            pl.BlockSpec((1, ft, d), lambda i, k, b, e, v: (e[i], f_idx(i, k, v), 0)),
        ],
        out_specs=rows_spec,
        scratch_shapes=[pltpu.VMEM((row_tile, d), BF16), pltpu.VMEM((row_tile, d), F32)],
    )
    return pl.pallas_call(
        functools.partial(_moe_expert_kernel, row_tile=row_tile),
        grid_spec=grid_spec,
        out_shape=jax.ShapeDtypeStruct(xs.shape, F32),
        compiler_params=pltpu.CompilerParams(
            dimension_semantics=("arbitrary", "arbitrary"), vmem_limit_bytes=VMEM_LIMIT_BYTES),
        name="moe_experts",
    )(blk, exp, valid, xs, w_gate, w_up, w_down)


def _moe_combine_kernel(x_ref, dest_ref, dest_next_ref, wts_ref, ys_ref, lng_ref, lnb_ref, out_ref,
                        ya0_ref, yb0_ref, ya1_ref, yb1_ref, dest_smem, sem_idx, sem0, sem1, *, tile):
    i = pl.program_id(0)
    last = pl.num_programs(0) - 1
    slots = ((ya0_ref, yb0_ref, sem0), (ya1_ref, yb1_ref, sem1))

    def fetch(idx_ref, ya_ref, yb_ref, sem):
        idx_copy = pltpu.make_async_copy(idx_ref.at[0], dest_smem, sem_idx)
        idx_copy.start()
        idx_copy.wait()

        def body(t, carry):
            _slab_copy(ys_ref, dest_smem[0, t], ya_ref, t, sem).start()
            _slab_copy(ys_ref, dest_smem[1, t], yb_ref, t, sem).start()
            return carry

        lax.fori_loop(0, tile, body, 0, unroll=8)

    def drain(ya_ref, yb_ref, sem):
        def body(t, carry):
            _slab_copy(ys_ref, 0, ya_ref, 0, sem).wait()
            _slab_copy(ys_ref, 0, yb_ref, 0, sem).wait()
            return carry

        lax.fori_loop(0, tile, body, 0, unroll=8)

    @pl.when(i == 0)
    def _():
        fetch(dest_ref, *slots[0])

    for parity in range(2):
        @pl.when(lax.rem(i, 2) == parity)
        def _(parity=parity):
            ya_ref, yb_ref, sem = slots[parity]

            @pl.when(i < last)
            def _():
                fetch(dest_next_ref, *slots[1 - parity])

            drain(ya_ref, yb_ref, sem)
            w = wts_ref[...]
            w1 = w[:, 0:1]
            w2 = w[:, 1:2]
            f = jnp.concatenate(
                [w1 * ya_ref[pl.ds(j, tile, stride=SLAB), :] + w2 * yb_ref[pl.ds(j, tile, stride=SLAB), :]
                 for j in range(SLAB)], axis=1)
            out_ref[...] = _layer_norm(DN_ALPHA * x_ref[...] + f, lng_ref[...], lnb_ref[...])


def _moe_combine(x2, dest, wts, ys, ln_g, ln_b, tile):
    n, d = x2.shape
    n_steps = n // tile
    tok = pl.BlockSpec((tile, d), lambda i: (i, 0))
    return pl.pallas_call(
        functools.partial(_moe_combine_kernel, tile=tile),
        grid=(n_steps,),
        in_specs=[
            tok,
            pl.BlockSpec((1, 2, tile), lambda i: (i, 0, 0)),
            pl.BlockSpec((1, 2, tile), lambda i: (jnp.minimum(i + 1, n_steps - 1), 0, 0)),
            pl.BlockSpec((tile, LANES), lambda i: (i, 0)),
            pl.BlockSpec(memory_space=pl.ANY),
            _const_spec((1, d)), _const_spec((1, d)),
        ],
        out_specs=tok,
        out_shape=jax.ShapeDtypeStruct((n, d), F32),
        scratch_shapes=[
            pltpu.VMEM((tile * SLAB, LANES), F32),
            pltpu.VMEM((tile * SLAB, LANES), F32),
            pltpu.VMEM((tile * SLAB, LANES), F32),
            pltpu.VMEM((tile * SLAB, LANES), F32),
            pltpu.SMEM((2, tile), jnp.int32),
            pltpu.SemaphoreType.DMA,
            pltpu.SemaphoreType.DMA,
            pltpu.SemaphoreType.DMA,
        ],
        compiler_params=pltpu.CompilerParams(
            dimension_semantics=("arbitrary",), vmem_limit_bytes=VMEM_LIMIT_BYTES),
        name="moe_combine",
    )(x2, dest, dest, wts, ys, ln_g, ln_b)


def _moe_plan(counts, eidx, slot, n_tiles, row_tile):
    padded = ((counts + row_tile - 1) // row_tile) * row_tile
    ends = jnp.cumsum(padded)
    offs = ends - padded
    dest = offs[eidx] + slot
    used_tiles = ends[-1] // row_tile
    steps = jnp.arange(n_tiles, dtype=jnp.int32)
    valid = (steps < used_tiles).astype(jnp.int32)
    clipped = jnp.minimum(steps, used_tiles - 1)
    exp = jnp.sum((clipped[:, None] * row_tile >= ends[None, :]).astype(jnp.int32), axis=1)
    blk = steps
    fill_lo = jnp.concatenate([offs + counts, ends[-1:]])
    fill_hi = jnp.concatenate([ends, jnp.full((1,), n_tiles * row_tile, jnp.int32)])
    return (dest.astype(jnp.int32), blk, exp.astype(jnp.int32), valid,
            fill_lo.astype(jnp.int32), fill_hi.astype(jnp.int32))


def _moe_ffn(x2, router, w_gate, w_up, w_down, ln_g, ln_b, tile, row_tile, ft):
    n, d = x2.shape
    assert d == SLAB * LANES
    router_t = jnp.pad(router.T, ((0, 2 * SLAB - N_EXPERTS), (0, 0)))
    eidx, slot, wts, counts = _moe_route(x2, router_t, tile)
    n_tiles = (2 * n) // row_tile + N_EXPERTS
    dest, blk, exp, valid, fill_lo, fill_hi = _moe_plan(counts[:N_EXPERTS, 0], eidx, slot, n_tiles, row_tile)
    xs = _moe_dispatch(x2, dest, fill_lo, fill_hi, n_tiles * row_tile, tile)
    ys = _moe_experts(xs, blk, exp, valid, w_gate, w_up, w_down, row_tile, ft)
    return _moe_combine(x2, dest, wts, ys, ln_g, ln_b, tile)


def _layer_weights(layer, w_in, conv_w, conv_b, dt_bias, a_log, d_skip, ssd_norm_w, ret_gn_w, ret_gn_b,
                   w_ret_br, w_ssd_br, w_out, ln1_g, ln1_b):
    d = w_in.shape[1]
    parts, start = [], 0
    for width in IN_SPLITS:
        parts.append(w_in[layer, :, start:start + width])
        start += width
    wq, wk, wv, wgt, wz, wxbc, wdt, wgr, wgs = parts
    half = RET_DK // 2
    perm = jnp.concatenate([jnp.arange(half) * 2, jnp.arange(half) * 2 + 1])
    wq = wq.reshape(d, RET_HEADS, RET_DK)[:, :, perm]
    wk = wk.reshape(d, RET_HEADS, RET_DK)[:, :, perm]
    wv = wv.reshape(d, RET_HEADS, RET_DV)
    wgt = wgt.reshape(d, RET_HEADS, RET_DV)
    w_ret = jnp.transpose(jnp.concatenate([wq, wk, wv, wgt], axis=2), (1, 0, 2)).astype(BF16)
    pad_h = LANES - SSD_HEADS
    return {
        "w_ret": w_ret,
        "w_gate_ret": wgr.astype(BF16),
        "w_ret_br": w_ret_br[layer].reshape(RET_HEADS, RET_DV, d).astype(BF16),
        "ret_gn_w": ret_gn_w[layer].reshape(RET_HEADS, 1, RET_DV),
        "ret_gn_b": ret_gn_b[layer].reshape(RET_HEADS, 1, RET_DV),
        "w_z": wz.astype(BF16),
        "w_xbc": wxbc.astype(BF16),
        "w_dt": jnp.pad(wdt, ((0, 0), (0, pad_h))).astype(BF16),
        "w_gate_ssd": wgs.astype(BF16),
        "w_ssd_br": w_ssd_br[layer].astype(BF16),
        "w_out": w_out[layer].astype(BF16),
        "conv_w": conv_w[layer],
        "conv_b": conv_b[layer].reshape(1, SSD_CONV_DIM),
        "dt_bias": jnp.pad(dt_bias[layer], (0, pad_h)).reshape(1, LANES),
        "a_log": jnp.pad(a_log[layer], (0, pad_h)).reshape(1, LANES),
        "d_skip": jnp.repeat(d_skip[layer], SSD_HEAD_DIM).reshape(1, SSD_INNER),
        "ssd_norm_w": ssd_norm_w[layer].reshape(1, SSD_INNER),
        "ln1_g": ln1_g[layer].reshape(1, d),
        "ln1_b": ln1_b[layer].reshape(1, d),
    }


def _forward(x, positions, w_in, conv_w, conv_b, dt_bias, a_log, d_skip, ssd_norm_w, ret_gn_w, ret_gn_b,
             w_ret_br, w_ssd_br, w_out, ln1_g, ln1_b, ln2_g, ln2_b, dense_w_gate, dense_w_up, dense_w_down,
             moe_router, moe_w_gate, moe_w_up, moe_w_down, *, ret_tile, ret_chunk, ssd_tile, ssd_chunk,
             ffn_tile, moe_row_tile, moe_ft):
    b, s, d = x.shape
    pos3 = positions.reshape(b, s, 1)
    for layer in range(w_in.shape[0]):
        lw = _layer_weights(layer, w_in, conv_w, conv_b, dt_bias, a_log, d_skip, ssd_norm_w, ret_gn_w,
                            ret_gn_b, w_ret_br, w_ssd_br, w_out, ln1_g, ln1_b)
        yret = _retention_branch(x, pos3, lw, ret_tile, ret_chunk)
        x = _ssd_branch(x, yret, lw, ssd_tile, ssd_chunk)
        x2 = x.reshape(b * s, d)
        g2 = ln2_g[layer].reshape(1, d)
        b2 = ln2_b[layer].reshape(1, d)
        i = layer // 2
        if layer % 2 == 0:
            x2 = _dense_ffn(x2, dense_w_gate[i].astype(BF16), dense_w_up[i].astype(BF16),
                            dense_w_down[i].astype(BF16), g2, b2, ffn_tile)
        else:
            x2 = _moe_ffn(x2, moe_router[i], moe_w_gate[i].astype(BF16), moe_w_up[i].astype(BF16),
                          moe_w_down[i].astype(BF16), g2, b2, ffn_tile, moe_row_tile, moe_ft)
        x = x2.reshape(b, s, d)
    return x


def kernel(x, positions, w_in, conv_w, conv_b, dt_bias, a_log, d_skip, ssd_norm_w, ret_gn_w, ret_gn_b, w_ret_br, w_ssd_br, w_out, ln1_g, ln1_b, ln2_g, ln2_b, dense_w_gate, dense_w_up, dense_w_down, moe_router, moe_w_gate, moe_w_up, moe_w_down):
    return _forward(x, positions, w_in, conv_w, conv_b, dt_bias, a_log, d_skip, ssd_norm_w, ret_gn_w, ret_gn_b,
                    w_ret_br, w_ssd_br, w_out, ln1_g, ln1_b, ln2_g, ln2_b, dense_w_gate, dense_w_up,
                    dense_w_down, moe_router, moe_w_gate, moe_w_up, moe_w_down,
                    ret_tile=RET_TILE, ret_chunk=RET_CHUNK, ssd_tile=SSD_TILE, ssd_chunk=SSD_CHUNK,
                    ffn_tile=FFN_TILE, moe_row_tile=MOE_ROW_TILE, moe_ft=MOE_FT)
```
